```python
import math
import jax, jax.numpy as jnp
from jax import lax
import numpy as np

D_MODEL = 1024
BATCH = 2
SEQ = 8192
DEPTH = 1
DEC_BATCH = 32
DEC_SEQ = 8
PAST_LEN = 8192
PAGE_SIZE = 128

N_META = 16
ATTN_HEADS = 4
HEAD_DIM = 64
V_DIM = 2 * HEAD_DIM
ATTN_WIDTH = ATTN_HEADS * V_DIM
CONV_WIDTH = D_MODEL - ATTN_WIDTH
QK_COLS = ATTN_HEADS * 2 * HEAD_DIM
IN_COLS = 2 * QK_COLS + ATTN_WIDTH + 3 * CONV_WIDTH
CONV_K = 3
D_FF = 2816
Q_BLOCK = 128
RMS_EPS = 1e-6
SUBLN_EPS = 1e-5
NEG_INF = -1e30

kernel_name = "hymba_diffattn_shortconv_macaron_step"


def rmsnorm(x, g, eps=RMS_EPS):
    xf = x.astype(jnp.float32)
    y = xf * lax.rsqrt(jnp.mean(xf * xf, axis=-1, keepdims=True) + eps) * g.astype(jnp.float32)
    return y.astype(x.dtype)


def swiglu(x, w_gate, w_up, w_down):
    return (jax.nn.silu(x @ w_gate) * (x @ w_up)) @ w_down


def diff_attend(q, k, v, q_pos, k_pos, lam):
    s = jnp.einsum('bqhcd,bkhcd->bhcqk', q.astype(jnp.float32), k.astype(jnp.float32)) * (HEAD_DIM ** -0.5)
    mask = k_pos[None, :] <= q_pos[:, None]
    s = jnp.where(mask, s, NEG_INF)
    p = jax.nn.softmax(s, axis=-1)
    a = p[:, :, 0] - lam * p[:, :, 1]
    return jnp.einsum('bhqk,bkhe->bqhe', a, v.astype(jnp.float32))


def attend_prompt(q, k, v, lam):
    b, t = q.shape[0], q.shape[1]
    s_len = t - N_META
    k_pos = jnp.arange(t)
    kf = k.astype(jnp.float32)
    vf = v.astype(jnp.float32)
    meta_pos = jnp.arange(N_META)
    o_meta = diff_attend(q[:, :N_META], kf[:, :N_META], vf[:, :N_META], meta_pos, meta_pos, lam)
    nb = s_len // Q_BLOCK
    q_blocks = jnp.moveaxis(q[:, N_META:].reshape(b, nb, Q_BLOCK, ATTN_HEADS, 2, HEAD_DIM), 1, 0)
    starts = N_META + jnp.arange(nb) * Q_BLOCK

    def one_block(args):
        qb, s0 = args
        return diff_attend(qb, kf, vf, s0 + jnp.arange(Q_BLOCK), k_pos, lam)

    o_real = lax.map(one_block, (q_blocks, starts))
    o_real = jnp.moveaxis(o_real, 0, 1).reshape(b, s_len, ATTN_HEADS, V_DIM)
    return jnp.concatenate([o_meta, o_real], axis=1)


def attend_sample(q, k, v, lam, k_past, v_past):
    past = k_past.shape[1]
    tq = q.shape[1]
    k_all = jnp.concatenate([k_past, k], axis=1)
    v_all = jnp.concatenate([v_past, v], axis=1)
    k_pos = jnp.arange(past + tq)
    q_pos = past + jnp.arange(tq)
    return diff_attend(q, k_all, v_all, q_pos, k_pos, lam)


def mixer(h, w_in, lq1, lk1, lq2, lk2, subln_g, conv_w, w_out, lambda_init, attend, conv_prev):
    b, t, _ = h.shape
    proj = h @ w_in
    q, k, v, gate_b, gate_c, hc = jnp.split(
        proj, [QK_COLS, 2 * QK_COLS, 2 * QK_COLS + ATTN_WIDTH,
               2 * QK_COLS + ATTN_WIDTH + CONV_WIDTH,
               2 * QK_COLS + ATTN_WIDTH + 2 * CONV_WIDTH], axis=-1)
    q = q.reshape(b, t, ATTN_HEADS, 2, HEAD_DIM)
    k = k.reshape(b, t, ATTN_HEADS, 2, HEAD_DIM)
    v = v.reshape(b, t, ATTN_HEADS, V_DIM)
    lam = (jnp.exp(jnp.sum(lq1.astype(jnp.float32) * lk1.astype(jnp.float32)))
           - jnp.exp(jnp.sum(lq2.astype(jnp.float32) * lk2.astype(jnp.float32))) + lambda_init)
    o = attend(q, k, v, lam)
    o = rmsnorm(o, subln_g, SUBLN_EPS) * (1.0 - lambda_init)
    o = o.reshape(b, t, ATTN_WIDTH).astype(h.dtype)
    u = gate_c * hc
    u_ext = jnp.concatenate([conv_prev.astype(u.dtype), u], axis=1)
    y = sum(conv_w[j] * u_ext[:, j:j + t] for j in range(CONV_K))
    c_out = gate_b * y
    out = jnp.concatenate([o, c_out], axis=-1) @ w_out
    return out, k, v, u_ext[:, -(CONV_K - 1):]


def layer(x, attend, conv_prev, lambda_init,
          f1_pre, f1_wg, f1_wu, f1_wd, f1_post,
          m_pre, w_in, lq1, lk1, lq2, lk2, subln_g, conv_w, w_out, m_post,
          f2_pre, f2_wg, f2_wu, f2_wd, f2_post):
    x = x + 0.5 * rmsnorm(swiglu(rmsnorm(x, f1_pre), f1_wg, f1_wu, f1_wd), f1_post)
    m, k, v, conv_new = mixer(rmsnorm(x, m_pre), w_in, lq1, lk1, lq2, lk2, subln_g, conv_w, w_out,
                              lambda_init, attend, conv_prev)
    x = x + rmsnorm(m, m_post)
    x = x + 0.5 * rmsnorm(swiglu(rmsnorm(x, f2_pre), f2_wg, f2_wu, f2_wd), f2_post)
    return x, k, v, conv_new


def setup_inputs(seed: int = 0) -> dict:
    key = jax.random.key(seed)
    ks = jax.random.split(key, 32)
    n_pages = PAST_LEN // PAGE_SIZE
    n_used = DEC_BATCH * n_pages
    n_pool = n_used + n_used // 4
    f32 = jnp.float32

    def nrm(k, shape, scale):
        return jax.random.normal(k, shape, f32) * scale

    def gain(k):
        return 1.0 + nrm(k, (DEPTH, D_MODEL), 0.05)

    page_table = jax.random.permutation(ks[0], n_pool)[:n_used].reshape(DEC_BATCH, n_pages).astype(jnp.int32)
    return {
        "x_prompt": nrm(ks[1], (BATCH, SEQ, D_MODEL), 1.0),
        "x_sample": nrm(ks[2], (DEC_BATCH, DEC_SEQ, D_MODEL), 1.0),
        "cache_k": nrm(ks[3], (DEPTH, n_pool, PAGE_SIZE, ATTN_HEADS, 2, HEAD_DIM), 1.0),
        "cache_v": nrm(ks[4], (DEPTH, n_pool, PAGE_SIZE, ATTN_HEADS, V_DIM), 1.0),
        "state_conv": nrm(ks[5], (DEPTH, DEC_BATCH, CONV_K - 1, CONV_WIDTH), 1.0),
        "page_table": page_table,
        "meta_tokens": nrm(ks[6], (N_META, D_MODEL), 1.0),
        "ffn1_pre_g": gain(ks[7]),
        "ffn1_w_gate": nrm(ks[8], (DEPTH, D_MODEL, D_FF), D_MODEL ** -0.5),
        "ffn1_w_up": nrm(ks[9], (DEPTH, D_MODEL, D_FF), D_MODEL ** -0.5),
        "ffn1_w_down": nrm(ks[10], (DEPTH, D_FF, D_MODEL), D_FF ** -0.5),
        "ffn1_post_g": gain(ks[11]),
        "mix_pre_g": gain(ks[12]),
        "w_in": nrm(ks[13], (DEPTH, D_MODEL, IN_COLS), D_MODEL ** -0.5),
        "lambda_q1": nrm(ks[14], (DEPTH, HEAD_DIM), 0.1),
        "lambda_k1": nrm(ks[15], (DEPTH, HEAD_DIM), 0.1),
        "lambda_q2": nrm(ks[16], (DEPTH, HEAD_DIM), 0.1),
        "lambda_k2": nrm(ks[17], (DEPTH, HEAD_DIM), 0.1),
        "subln_g": 1.0 + nrm(ks[18], (DEPTH, V_DIM), 0.05),
        "conv_w": nrm(ks[19], (DEPTH, CONV_K, CONV_WIDTH), CONV_K ** -0.5),
        "w_out": nrm(ks[20], (DEPTH, D_MODEL, D_MODEL), D_MODEL ** -0.5),
        "mix_post_g": gain(ks[21]),
        "ffn2_pre_g": gain(ks[22]),
        "ffn2_w_gate": nrm(ks[23], (DEPTH, D_MODEL, D_FF), D_MODEL ** -0.5),
        "ffn2_w_up": nrm(ks[24], (DEPTH, D_MODEL, D_FF), D_MODEL ** -0.5),
        "ffn2_w_down": nrm(ks[25], (DEPTH, D_FF, D_MODEL), D_FF ** -0.5),
        "ffn2_post_g": gain(ks[26]),
    }


def reference(x_prompt, x_sample, cache_k, cache_v, state_conv, page_table, meta_tokens,
              ffn1_pre_g, ffn1_w_gate, ffn1_w_up, ffn1_w_down, ffn1_post_g,
              mix_pre_g, w_in, lambda_q1, lambda_k1, lambda_q2, lambda_k2, subln_g, conv_w, w_out, mix_post_g,
              ffn2_pre_g, ffn2_w_gate, ffn2_w_up, ffn2_w_down, ffn2_post_g):
    b = x_prompt.shape[0]
    db, n_pages = page_table.shape
    past = n_pages * PAGE_SIZE
    meta = jnp.broadcast_to(meta_tokens.astype(x_prompt.dtype)[None], (b, N_META, D_MODEL))
    xp = jnp.concatenate([meta, x_prompt], axis=1)
    xs = x_sample
    kp_l, vp_l, cp_l, ks_l, vs_l, cs_l = [], [], [], [], [], []
    for l in range(DEPTH):
        lambda_init = 0.8 - 0.6 * math.exp(-0.3 * l)
        params = (ffn1_pre_g[l], ffn1_w_gate[l], ffn1_w_up[l], ffn1_w_down[l], ffn1_post_g[l],
                  mix_pre_g[l], w_in[l], lambda_q1[l], lambda_k1[l], lambda_q2[l], lambda_k2[l],
                  subln_g[l], conv_w[l], w_out[l], mix_post_g[l],
                  ffn2_pre_g[l], ffn2_w_gate[l], ffn2_w_up[l], ffn2_w_down[l], ffn2_post_g[l])
        conv0 = jnp.zeros((b, CONV_K - 1, CONV_WIDTH), xp.dtype)
        xp, kp, vp, cp = layer(xp, attend_prompt, conv0, lambda_init, *params)
        k_past = cache_k[l][page_table].reshape(db, past, ATTN_HEADS, 2, HEAD_DIM)
        v_past = cache_v[l][page_table].reshape(db, past, ATTN_HEADS, V_DIM)
        att_s = lambda q, k, v, lam, kp_=k_past, vp_=v_past: attend_sample(q, k, v, lam, kp_, vp_)
        xs, ksn, vsn, csn = layer(xs, att_s, state_conv[l], lambda_init, *params)
        kp_l.append(kp); vp_l.append(vp); cp_l.append(cp)
        ks_l.append(ksn); vs_l.append(vsn); cs_l.append(csn)
    y_prompt = xp[:, N_META:]
    y_sample = xs
    k_prompt_new = jnp.stack(kp_l)
    v_prompt_new = jnp.stack(vp_l)
    conv_prompt_new = jnp.stack(cp_l)
    k_sample_new = jnp.stack(ks_l)
    v_sample_new = jnp.stack(vs_l)
    conv_sample_new = jnp.stack(cs_l)
    return (y_prompt, y_sample, k_prompt_new, v_prompt_new, conv_prompt_new, k_sample_new, v_sample_new, conv_sample_new)
```

```python
import functools
import math

import jax
import jax.numpy as jnp
from jax import lax
from jax.experimental import pallas as pl
from jax.experimental.pallas import tpu as pltpu

F32 = jnp.float32
BF16 = jnp.bfloat16

D_MODEL = 1024
D_FF = 2816
N_META = 16
HEADS = 4
HEAD_DIM = 64
V_DIM = 2 * HEAD_DIM
ATTN_W = HEADS * V_DIM
CONV_W = D_MODEL - ATTN_W
IN_COLS = 3 * ATTN_W + 3 * CONV_W
PAGE = 128
RMS_EPS = 1e-6
SUBLN_EPS = 1e-5
NEG_INF = -1e30
LAMBDA_INIT = 0.8 - 0.6 * math.exp(-0.3 * 0)
QK_SCALE = HEAD_DIM ** -0.5

SUBLANES = 8
LANES = 128
VMEM_LIMIT = 56 * 1024 * 1024

ROW_TILE = 512
FF_CHUNK = 1408
Q_TILE = 256
KV_TILE = 256
PAGES_PER_STEP = 8


def _rms(x, g, eps):
    ms = jnp.mean(x * x, axis=-1, keepdims=True)
    return x * lax.rsqrt(ms + eps) * g


def _dot(a, b):
    return jnp.dot(a, b, preferred_element_type=F32)


def _dot_nt(a, b):
    return lax.dot_general(a, b, (((1,), (1,)), ((), ())), preferred_element_type=F32)


def _ffn_half_step(x, pre_g, wg_ref, wu_ref, wd_ref, post_g):
    h = _rms(x, pre_g, RMS_EPS).astype(BF16)
    acc = None
    for c in range(D_FF // FF_CHUNK):
        sl = slice(c * FF_CHUNK, (c + 1) * FF_CHUNK)
        g = _dot(h, wg_ref[:, sl])
        u = _dot(h, wu_ref[:, sl])
        a = (g * jax.nn.sigmoid(g) * u).astype(BF16)
        d = _dot(a, wd_ref[sl, :])
        acc = d if acc is None else acc + d
    return x + 0.5 * _rms(acc, post_g, RMS_EPS)


def _ffn_kernel(x_ref, pre_ref, wg_ref, wu_ref, wd_ref, post_ref, o_ref):
    o_ref[...] = _ffn_half_step(x_ref[...], pre_ref[...], wg_ref, wu_ref, wd_ref, post_ref[...])


def _mixout_ffn_kernel(x_ref, a_ref, c_ref, wo_ref, mpost_ref,
                       pre_ref, wg_ref, wu_ref, wd_ref, post_ref, o_ref):
    m = _dot(a_ref[...].astype(BF16), wo_ref[:ATTN_W, :]) + _dot(c_ref[...], wo_ref[ATTN_W:, :])
    x2 = x_ref[...] + _rms(m, mpost_ref[...], RMS_EPS)
    o_ref[...] = _ffn_half_step(x2, pre_ref[...], wg_ref, wu_ref, wd_ref, post_ref[...])


def _const_spec(shape):
    return pl.BlockSpec(shape, lambda *_: (0,) * len(shape), pipeline_mode=pl.Buffered(1))


def _row_spec(tile, cols):
    return pl.BlockSpec((tile, cols), lambda i: (i, 0))


def _ffn_weight_specs():
    return [_const_spec((1, D_MODEL)), _const_spec((D_MODEL, D_FF)), _const_spec((D_MODEL, D_FF)),
            _const_spec((D_FF, D_MODEL)), _const_spec((1, D_MODEL))]


def _row_params():
    return pltpu.CompilerParams(dimension_semantics=("arbitrary",), vmem_limit_bytes=VMEM_LIMIT)


def _ffn_call(x, pre_g, wg, wu, wd, post_g, tile):
    rows = x.shape[0]
    return pl.pallas_call(
        _ffn_kernel,
        grid=(rows // tile,),
        in_specs=[_row_spec(tile, D_MODEL)] + _ffn_weight_specs(),
        out_specs=_row_spec(tile, D_MODEL),
        out_shape=jax.ShapeDtypeStruct((rows, D_MODEL), F32),
        compiler_params=_row_params(),
        name="ffn1",
    )(x, pre_g, wg, wu, wd, post_g)


def _mixout_ffn_call(x1, attn, conv, wo, mpost_g, pre_g, wg, wu, wd, post_g, tile):
    rows = x1.shape[0]
    return pl.pallas_call(
        _mixout_ffn_kernel,
        grid=(rows // tile,),
        in_specs=[_row_spec(tile, D_MODEL), _row_spec(tile, ATTN_W), _row_spec(tile, CONV_W),
                  _const_spec((D_MODEL, D_MODEL)), _const_spec((1, D_MODEL))] + _ffn_weight_specs(),
        out_specs=_row_spec(tile, D_MODEL),
        out_shape=jax.ShapeDtypeStruct((rows, D_MODEL), F32),
        compiler_params=_row_params(),
        name="mixout_ffn2",
    )(x1, attn, conv, wo, mpost_g, pre_g, wg, wu, wd, post_g)


def _in_projection(x, g, w_ref):
    h = _rms(x, g, RMS_EPS).astype(BF16)
    p = _dot(h, w_ref[...])
    q = p[:, :ATTN_W] * QK_SCALE
    k = p[:, ATTN_W:2 * ATTN_W]
    v = p[:, 2 * ATTN_W:3 * ATTN_W]
    o = 3 * ATTN_W
    gate_b = p[:, o:o + CONV_W]
    u = p[:, o + CONV_W:o + 2 * CONV_W] * p[:, o + 2 * CONV_W:o + 3 * CONV_W]
    return q, k, v, gate_b, u


def _inproj_main_kernel(tiles_per_seq, x_ref, g_ref, w_ref, cw_ref, init_ref,
                        q_ref, kf_ref, vf_ref, kb_ref, vb_ref, c_ref, tail_ref, ubuf):
    tile = x_ref.shape[0]

    @pl.when(pl.program_id(0) % tiles_per_seq == 0)
    def _():
        ubuf[0:SUBLANES, :] = init_ref[...]

    q, k, v, gate_b, u = _in_projection(x_ref[...], g_ref[...], w_ref)
    q_ref[...] = q.astype(BF16)
    kf_ref[...] = k
    vf_ref[...] = v
    kb_ref[...] = k.astype(BF16)
    vb_ref[...] = v.astype(BF16)
    ubuf[SUBLANES:SUBLANES + tile, :] = u
    u1 = ubuf[SUBLANES - 1:SUBLANES - 1 + tile, :]
    u2 = ubuf[SUBLANES - 2:SUBLANES - 2 + tile, :]
    y = cw_ref[0:1, :] * u2 + cw_ref[1:2, :] * u1 + cw_ref[2:3, :] * u
    c_ref[...] = (gate_b * y).astype(BF16)
    tail_ref[...] = u[tile - 2:, :]
    ubuf[0:SUBLANES, :] = u[tile - SUBLANES:, :]


def _inproj_extra_kernel(x_ref, g_ref, w_ref, cw_ref, p1_ref, p2_ref, m1_ref, m2_ref,
                         q_ref, kf_ref, vf_ref, c_ref, u_ref, ubuf):
    tile = x_ref.shape[0]
    q, k, v, gate_b, u = _in_projection(x_ref[...], g_ref[...], w_ref)
    q_ref[...] = q
    kf_ref[...] = k
    vf_ref[...] = v
    u_ref[...] = u
    ubuf[0:SUBLANES, :] = jnp.zeros((SUBLANES, CONV_W), F32)
    ubuf[SUBLANES:SUBLANES + tile, :] = u
    u1 = jnp.where(m1_ref[...] > 0.5, p1_ref[...], ubuf[SUBLANES - 1:SUBLANES - 1 + tile, :])
    u2 = jnp.where(m2_ref[...] > 0.5, p2_ref[...], ubuf[SUBLANES - 2:SUBLANES - 2 + tile, :])
    y = cw_ref[0:1, :] * u2 + cw_ref[1:2, :] * u1 + cw_ref[2:3, :] * u
    c_ref[...] = (gate_b * y).astype(BF16)


def _inproj_main_call(x1, g, w_in, conv_w, init, tile, rows_per_seq):
    rows = x1.shape[0]
    tiles_per_seq = rows_per_seq // tile
    n_seq = rows // rows_per_seq
    sd = jax.ShapeDtypeStruct
    return pl.pallas_call(
        functools.partial(_inproj_main_kernel, tiles_per_seq),
        grid=(rows // tile,),
        in_specs=[_row_spec(tile, D_MODEL), _const_spec((1, D_MODEL)), _const_spec((D_MODEL, IN_COLS)),
                  _const_spec((3, CONV_W)), _const_spec((SUBLANES, CONV_W))],
        out_specs=[_row_spec(tile, ATTN_W)] * 5 + [_row_spec(tile, CONV_W),
                   pl.BlockSpec((None, 2, CONV_W), lambda i: (i // tiles_per_seq, 0, 0))],
        out_shape=[sd((rows, ATTN_W), BF16), sd((rows, ATTN_W), F32), sd((rows, ATTN_W), F32),
                   sd((rows, ATTN_W), BF16), sd((rows, ATTN_W), BF16), sd((rows, CONV_W), BF16),
                   sd((n_seq, 2, CONV_W), F32)],
        scratch_shapes=[pltpu.VMEM((SUBLANES + tile, CONV_W), F32)],
        compiler_params=_row_params(),
        name="inproj_main",
    )(x1, g, w_in, conv_w, init)


def _inproj_extra_call(x1, g, w_in, conv_w, p1, p2, m1, m2):
    rows = x1.shape[0]
    sd = jax.ShapeDtypeStruct
    full = lambda cols: pl.BlockSpec((rows, cols), lambda i: (0, 0))
    return pl.pallas_call(
        _inproj_extra_kernel,
        grid=(1,),
        in_specs=[full(D_MODEL), _const_spec((1, D_MODEL)), _const_spec((D_MODEL, IN_COLS)),
                  _const_spec((3, CONV_W))] + [full(CONV_W)] * 4,
        out_specs=[full(ATTN_W)] * 3 + [full(CONV_W)] * 2,
        out_shape=[sd((rows, ATTN_W), F32)] * 3 + [sd((rows, CONV_W), BF16), sd((rows, CONV_W), F32)],
        scratch_shapes=[pltpu.VMEM((SUBLANES + rows, CONV_W), F32)],
        compiler_params=_row_params(),
        name="inproj_extra",
    )(x1, g, w_in, conv_w, p1, p2, m1, m2)


def _lambda(lq1_ref, lk1_ref, lq2_ref, lk2_ref):
    a = jnp.exp(jnp.sum(lq1_ref[...] * lk1_ref[...], axis=-1, keepdims=True))
    b = jnp.exp(jnp.sum(lq2_ref[...] * lk2_ref[...], axis=-1, keepdims=True))
    return a - b + LAMBDA_INIT


def _head_out(o1, o2, lam, g):
    o = o1 - lam * o2
    return _rms(o, g, SUBLN_EPS) * (1.0 - LAMBDA_INIT)


def _stack_maps(q):
    lane = lax.broadcasted_iota(jnp.int32, q.shape, 1)
    zero = jnp.zeros_like(q)
    return jnp.concatenate([jnp.where(lane < HEAD_DIM, q, zero), jnp.where(lane >= HEAD_DIM, q, zero)], axis=0)


def _prompt_attn_kernel(q_ref, k_ref, v_ref, km_ref, vm_ref, lq1_ref, lk1_ref, lq2_ref, lk2_ref, g_ref,
                        o_ref, m_sc, l_sc, acc_sc):
    tq = q_ref.shape[0]
    i = pl.program_id(2)
    qs = _stack_maps(q_ref[...])

    def update(s, v, first):
        m_cur = jnp.max(s, axis=-1, keepdims=True)
        if first:
            m_new = m_cur
            p = jnp.exp(s - m_new)
            l_sc[...] = jnp.sum(p, axis=-1, keepdims=True)
            acc_sc[...] = _dot(p.astype(BF16), v)
        else:
            m_prev = m_sc[...]
            m_new = jnp.maximum(m_prev, m_cur)
            alpha = jnp.exp(m_prev - m_new)
            p = jnp.exp(s - m_new)
            l_sc[...] = alpha * l_sc[...] + jnp.sum(p, axis=-1, keepdims=True)
            acc_sc[...] = alpha * acc_sc[...] + _dot(p.astype(BF16), v)
        m_sc[...] = m_new

    s = _dot_nt(qs, km_ref[...])
    col = lax.broadcasted_iota(jnp.int32, s.shape, 1)
    update(jnp.where(col < N_META, s, NEG_INF), vm_ref[...], True)

    def full_block(j, carry):
        start = pl.multiple_of(j * KV_TILE, KV_TILE)
        update(_dot_nt(qs, k_ref[pl.ds(start, KV_TILE), :]), v_ref[pl.ds(start, KV_TILE), :], False)
        return carry

    lax.fori_loop(0, i, full_block, 0)

    start = pl.multiple_of(i * KV_TILE, KV_TILE)
    s = _dot_nt(qs, k_ref[pl.ds(start, KV_TILE), :])
    row = lax.broadcasted_iota(jnp.int32, s.shape, 0)
    col = lax.broadcasted_iota(jnp.int32, s.shape, 1)
    qrow = jnp.where(row >= tq, row - tq, row)
    update(jnp.where(col <= qrow, s, NEG_INF), v_ref[pl.ds(start, KV_TILE), :], False)

    o = acc_sc[...] / l_sc[...]
    lam = _lambda(lq1_ref, lk1_ref, lq2_ref, lk2_ref)
    o_ref[...] = _head_out(o[:tq], o[tq:], lam, g_ref[...]).astype(o_ref.dtype)


def _prompt_attn_call(q, k, v, km, vm, lq1, lk1, lq2, lk2, subln_g):
    b, s, _ = q.shape
    small = lambda shape: pl.BlockSpec(shape, lambda b_, h, i: (0,) * len(shape))
    return pl.pallas_call(
        _prompt_attn_kernel,
        grid=(b, HEADS, s // Q_TILE),
        in_specs=[pl.BlockSpec((None, Q_TILE, V_DIM), lambda b_, h, i: (b_, i, h)),
                  pl.BlockSpec((None, s, V_DIM), lambda b_, h, i: (b_, 0, h)),
                  pl.BlockSpec((None, s, V_DIM), lambda b_, h, i: (b_, 0, h)),
                  pl.BlockSpec((PAGE, V_DIM), lambda b_, h, i: (0, h)),
                  pl.BlockSpec((PAGE, V_DIM), lambda b_, h, i: (0, h)),
                  small((1, HEAD_DIM)), small((1, HEAD_DIM)), small((1, HEAD_DIM)), small((1, HEAD_DIM)),
                  small((1, V_DIM))],
        out_specs=pl.BlockSpec((None, Q_TILE, V_DIM), lambda b_, h, i: (b_, i, h)),
        out_shape=jax.ShapeDtypeStruct((b, s, ATTN_W), BF16),
        scratch_shapes=[pltpu.VMEM((2 * Q_TILE, 1), F32), pltpu.VMEM((2 * Q_TILE, 1), F32),
                        pltpu.VMEM((2 * Q_TILE, V_DIM), F32)],
        compiler_params=pltpu.CompilerParams(dimension_semantics=("arbitrary",) * 3,
                                             vmem_limit_bytes=VMEM_LIMIT),
        name="prompt_attn",
    )(q, k, v, km, vm, lq1, lk1, lq2, lk2, subln_g)


def _meta_attn_kernel(q_ref, k_ref, v_ref, lq1_ref, lk1_ref, lq2_ref, lk2_ref, g_ref, o_ref):
    t = q_ref.shape[0]
    qs = _stack_maps(q_ref[...].astype(BF16))
    s = _dot_nt(qs, k_ref[...].astype(BF16))
    row = lax.broadcasted_iota(jnp.int32, s.shape, 0)
    col = lax.broadcasted_iota(jnp.int32, s.shape, 1)
    qrow = jnp.where(row >= t, row - t, row)
    s = jnp.where(col <= qrow, s, NEG_INF)
    p = jnp.exp(s - jnp.max(s, axis=-1, keepdims=True))
    o = _dot(p.astype(BF16), v_ref[...].astype(BF16)) / jnp.sum(p, axis=-1, keepdims=True)
    lam = _lambda(lq1_ref, lk1_ref, lq2_ref, lk2_ref)
    o_ref[...] = _head_out(o[:t], o[t:], lam, g_ref[...])


def _meta_attn_call(q, k, v, lq1, lk1, lq2, lk2, subln_g):
    small = lambda shape: pl.BlockSpec(shape, lambda h: (0,) * len(shape))
    head = pl.BlockSpec((N_META, V_DIM), lambda h: (0, h))
    return pl.pallas_call(
        _meta_attn_kernel,
        grid=(HEADS,),
        in_specs=[head, head, head, small((1, HEAD_DIM)), small((1, HEAD_DIM)), small((1, HEAD_DIM)),
                  small((1, HEAD_DIM)), small((1, V_DIM))],
        out_specs=head,
        out_shape=jax.ShapeDtypeStruct((N_META, ATTN_W), F32),
        compiler_params=pltpu.CompilerParams(dimension_semantics=("arbitrary",)),
        name="meta_attn",
    )(q, k, v, lq1, lk1, lq2, lk2, subln_g)


def _sample_attn_kernel(n_pages, pt_ref, q_ref, kn_ref, vn_ref, lq1_ref, lk1_ref, lq2_ref, lk2_ref, g_ref,
                        *rest):
    k_refs = rest[:n_pages]
    v_refs = rest[n_pages:2 * n_pages]
    o_ref = rest[2 * n_pages]
    qbd_sc, m_sc, l_sc, acc_sc = rest[2 * n_pages + 1:]
    j = pl.program_id(1)
    t = q_ref.shape[0]
    n_cols = HEADS * 2 * t

    @pl.when(j == 0)
    def _():
        q = q_ref[...]
        qt = jnp.concatenate([q] * (LANES // t), axis=0)
        row = lax.broadcasted_iota(jnp.int32, qt.shape, 0)
        col = lax.broadcasted_iota(jnp.int32, qt.shape, 1)
        keep = (row < n_cols) & (col // HEAD_DIM == row // t)
        qbd_sc[...] = jnp.where(keep, qt, 0.0).astype(BF16)
        m_sc[...] = jnp.full(m_sc.shape, NEG_INF, F32)
        l_sc[...] = jnp.zeros(l_sc.shape, F32)
        acc_sc[...] = jnp.zeros(acc_sc.shape, F32)

    eye_r = lax.broadcasted_iota(jnp.int32, (LANES, LANES), 0)
    eye_c = lax.broadcasted_iota(jnp.int32, (LANES, LANES), 1)

    def to_col(x_row):
        return jnp.sum(jnp.where(eye_r == eye_c, x_row, 0.0), axis=1, keepdims=True)

    def accumulate(kb, vb, mask):
        st = _dot_nt(kb, qbd_sc[...])
        if mask is not None:
            st = jnp.where(mask, st, NEG_INF)
        m_prev = m_sc[...]
        m_new = jnp.maximum(m_prev, jnp.max(st, axis=0, keepdims=True))
        alpha = jnp.exp(m_prev - m_new)
        pt = jnp.exp(st - m_new)
        l_sc[...] = alpha * l_sc[...] + jnp.sum(pt, axis=0, keepdims=True)
        m_sc[...] = m_new
        pv = _dot(pt.T.astype(BF16), vb)
        acc_sc[...] = acc_sc[...] * to_col(alpha) + pv

    kb = jnp.concatenate([r[...].astype(BF16) for r in k_refs], axis=0)
    vb = jnp.concatenate([r[...].astype(BF16) for r in v_refs], axis=0)
    accumulate(kb, vb, None)

    @pl.when(j == pl.num_programs(1) - 1)
    def _():
        pad = jnp.zeros((PAGE - t, ATTN_W), F32)
        kn = jnp.concatenate([kn_ref[...], pad], axis=0).astype(BF16)
        vn = jnp.concatenate([vn_ref[...], pad], axis=0).astype(BF16)
        key = lax.broadcasted_iota(jnp.int32, (PAGE, LANES), 0)
        c = lax.broadcasted_iota(jnp.int32, (PAGE, LANES), 1)
        accumulate(kn, vn, (key < t) & (key <= c % t))
        o = acc_sc[...] / to_col(l_sc[...])
        lam = _lambda(lq1_ref, lk1_ref, lq2_ref, lk2_ref)
        for h in range(HEADS):
            r0 = h * 2 * t
            oh = o[:, h * V_DIM:(h + 1) * V_DIM]
            o_ref[:, h * V_DIM:(h + 1) * V_DIM] = _head_out(oh[r0:r0 + t], oh[r0 + t:r0 + 2 * t], lam, g_ref[...])


def _sample_attn_call(page_table, q, k_new, v_new, cache_k, cache_v, lq1, lk1, lq2, lk2, subln_g, row0, t):
    n_seq, n_pages = page_table.shape
    pps = PAGES_PER_STEP
    steps = n_pages // pps
    blk0 = row0 // t
    seq_spec = pl.BlockSpec((t, ATTN_W), lambda b, j, pt: (blk0 + b, 0))
    small = lambda shape: pl.BlockSpec(shape, lambda b, j, pt: (0,) * len(shape))

    def page_spec(r):
        return pl.BlockSpec((None, PAGE, ATTN_W), lambda b, j, pt: (pt[b * n_pages + j * pps + r], 0, 0))

    grid_spec = pltpu.PrefetchScalarGridSpec(
        num_scalar_prefetch=1,
        grid=(n_seq, steps),
        in_specs=[seq_spec, seq_spec, seq_spec, small((1, HEAD_DIM)), small((1, HEAD_DIM)),
                  small((1, HEAD_DIM)), small((1, HEAD_DIM)), small((1, V_DIM))]
                 + [page_spec(r) for r in range(pps)] * 2,
        out_specs=pl.BlockSpec((t, ATTN_W), lambda b, j, pt: (b, 0)),
        scratch_shapes=[pltpu.VMEM((LANES, ATTN_W), BF16), pltpu.VMEM((1, LANES), F32),
                        pltpu.VMEM((1, LANES), F32), pltpu.VMEM((LANES, ATTN_W), F32)],
    )
    return pl.pallas_call(
        functools.partial(_sample_attn_kernel, pps),
        grid_spec=grid_spec,
        out_shape=jax.ShapeDtypeStruct((n_seq * t, ATTN_W), F32),
        compiler_params=pltpu.CompilerParams(dimension_semantics=("arbitrary", "arbitrary"),
                                             vmem_limit_bytes=VMEM_LIMIT),
        name="sample_attn",
    )(page_table.reshape(-1), q, k_new, v_new, lq1, lk1, lq2, lk2, subln_g,
      *([cache_k] * pps), *([cache_v] * pps))


def kernel(x_prompt, x_sample, cache_k, cache_v, state_conv, page_table, meta_tokens, ffn1_pre_g, ffn1_w_gate, ffn1_w_up, ffn1_w_down, ffn1_post_g, mix_pre_g, w_in, lambda_q1, lambda_k1, lambda_q2, lambda_k2, subln_g, conv_w, w_out, mix_post_g, ffn2_pre_g, ffn2_w_gate, ffn2_w_up, ffn2_w_down, ffn2_post_g):
    batch, seq, _ = x_prompt.shape
    n_seq, t_new, _ = x_sample.shape
    n_pool = cache_k.shape[1]
    l = 0

    bf = lambda w: w[l].astype(BF16)
    f1 = (ffn1_pre_g, bf(ffn1_w_gate), bf(ffn1_w_up), bf(ffn1_w_down), ffn1_post_g)
    f2 = (ffn2_pre_g, bf(ffn2_w_gate), bf(ffn2_w_up), bf(ffn2_w_down), ffn2_post_g)
    w_in_b, w_out_b = bf(w_in), bf(w_out)
    lam_args = (lambda_q1, lambda_k1, lambda_q2, lambda_k2, subln_g)
    ck = cache_k[l].reshape(n_pool, PAGE, ATTN_W)
    cv = cache_v[l].reshape(n_pool, PAGE, ATTN_W)

    n_s = n_seq * t_new
    n_x = N_META + n_s
    xe = jnp.concatenate([meta_tokens, x_sample.reshape(n_s, D_MODEL)], axis=0)
    st = state_conv[l]
    zpad = lambda a, n: jnp.concatenate([a, jnp.zeros((n_seq, n, CONV_W), F32)], axis=1).reshape(n_s, CONV_W)
    zmeta = jnp.zeros((N_META, CONV_W), F32)
    p1 = jnp.concatenate([zmeta, zpad(st[:, 1:2], t_new - 1)], axis=0)
    p2 = jnp.concatenate([zmeta, zpad(st, t_new - 2)], axis=0)
    r = jnp.arange(n_x)[:, None]
    in_sample = r >= N_META
    pos = jnp.where(in_sample, (r - N_META) % t_new, r)
    m1 = jnp.broadcast_to((pos < 1).astype(F32), (n_x, CONV_W))
    m2 = jnp.broadcast_to((pos < 2).astype(F32), (n_x, CONV_W))

    xe1 = _ffn_call(xe, *f1, tile=n_x)
    qe, ke, ve, ce, ue = _inproj_extra_call(xe1, mix_pre_g, w_in_b, conv_w[l], p1, p2, m1, m2)
    ae_meta = _meta_attn_call(qe[:N_META], ke[:N_META], ve[:N_META], *lam_args)
    ae_s = _sample_attn_call(page_table, qe, ke, ve, ck, cv, *lam_args, row0=N_META, t=t_new)
    ae = jnp.concatenate([ae_meta, ae_s], axis=0)
    ye = _mixout_ffn_call(xe1, ae, ce, w_out_b, mix_post_g, *f2, tile=n_x)

    rows = batch * seq
    xp = x_prompt.reshape(rows, D_MODEL)
    xp1 = _ffn_call(xp, *f1, tile=ROW_TILE)
    init = jnp.concatenate([jnp.zeros((SUBLANES - 2, CONV_W), F32), ue[N_META - 2:N_META]], axis=0)
    qp, kpf, vpf, kpb, vpb, cp, tail = _inproj_main_call(xp1, mix_pre_g, w_in_b, conv_w[l], init,
                                                        tile=ROW_TILE, rows_per_seq=seq)
    meta_pad = lambda a: jnp.concatenate([a[:N_META], jnp.zeros((PAGE - N_META, ATTN_W), F32)], axis=0).astype(BF16)
    shp = (batch, seq, ATTN_W)
    ap = _prompt_attn_call(qp.reshape(shp), kpb.reshape(shp), vpb.reshape(shp), meta_pad(ke), meta_pad(ve),
                           *lam_args)
    yp = _mixout_ffn_call(xp1, ap.reshape(rows, ATTN_W), cp, w_out_b, mix_post_g, *f2, tile=ROW_TILE)

    y_prompt = yp.reshape(batch, seq, D_MODEL)
    y_sample = ye[N_META:].reshape(n_seq, t_new, D_MODEL)

    def with_meta(meta_rows, main):
        m = jnp.broadcast_to(meta_rows[None], (batch, N_META, ATTN_W))
        return jnp.concatenate([m, main.reshape(batch, seq, ATTN_W)], axis=1)

    k_prompt_new = with_meta(ke[:N_META], kpf).reshape(1, batch, N_META + seq, HEADS, 2, HEAD_DIM)
    v_prompt_new = with_meta(ve[:N_META], vpf).reshape(1, batch, N_META + seq, HEADS, V_DIM)
    conv_prompt_new = tail[None]
    k_sample_new = ke[N_META:].reshape(1, n_seq, t_new, HEADS, 2, HEAD_DIM)
    v_sample_new = ve[N_META:].reshape(1, n_seq, t_new, HEADS, V_DIM)
    conv_sample_new = ue[N_META:].reshape(n_seq, t_new, CONV_W)[None, :, t_new - 2:, :]
    return (y_prompt, y_sample, k_prompt_new, v_prompt_new, conv_prompt_new,
            k_sample_new, v_sample_new, conv_sample_new)
```

```python
import functools
import math

import jax
import jax.numpy as jnp
from jax import lax
from jax.experimental import pallas as pl
from jax.experimental.pallas import tpu as pltpu

F32 = jnp.float32
BF16 = jnp.bfloat16

D_MODEL = 1024
D_FF = 2816
N_META = 16
HEADS = 4
HEAD_DIM = 64
V_DIM = 2 * HEAD_DIM
ATTN_W = HEADS * V_DIM
CONV_W = D_MODEL - ATTN_W
IN_COLS = 3 * ATTN_W + 3 * CONV_W
PAGE = 128
RMS_EPS = 1e-6
SUBLN_EPS = 1e-5
NEG_INF = -1e30
LAMBDA_INIT = 0.8 - 0.6 * math.exp(-0.3 * 0)
QK_SCALE = HEAD_DIM ** -0.5
LOG2E = math.log2(math.e)

SUBLANES = 8
LANES = 128
VMEM_LIMIT = 56 * 1024 * 1024

ROW_TILE = 512
FF_CHUNK = 1408
Q_TILE = 256
KV_TILE = 256
ROW_CHUNK = 128
PAGES_PER_STEP = 16


def _rms(x, g, eps):
    ms = jnp.mean(x * x, axis=-1, keepdims=True)
    return x * lax.rsqrt(ms + eps) * g


def _dot(a, b):
    return jnp.dot(a, b, preferred_element_type=F32)


def _dot_nt(a, b):
    return lax.dot_general(a, b, (((1,), (1,)), ((), ())), preferred_element_type=F32)


def _ffn_half_step(x, pre_g, wg_ref, wu_ref, wd_ref, post_g):
    h = _rms(x, pre_g, RMS_EPS).astype(BF16)
    acc = None
    for c in range(D_FF // FF_CHUNK):
        sl = slice(c * FF_CHUNK, (c + 1) * FF_CHUNK)
        g = _dot(h, wg_ref[:, sl])
        u = _dot(h, wu_ref[:, sl])
        a = (g * jax.nn.sigmoid(g) * u).astype(BF16)
        d = _dot(a, wd_ref[sl, :])
        acc = d if acc is None else acc + d
    return x + 0.5 * _rms(acc, post_g, RMS_EPS)


def _ffn_kernel(x_ref, pre_ref, wg_ref, wu_ref, wd_ref, post_ref, o_ref):
    o_ref[...] = _ffn_half_step(x_ref[...], pre_ref[...], wg_ref, wu_ref, wd_ref, post_ref[...])


def _mixout_ffn_kernel(x_ref, a_ref, c_ref, wo_ref, mpost_ref,
                       pre_ref, wg_ref, wu_ref, wd_ref, post_ref, o_ref):
    m = _dot(a_ref[...].astype(BF16), wo_ref[:ATTN_W, :]) + _dot(c_ref[...], wo_ref[ATTN_W:, :])
    x2 = x_ref[...] + _rms(m, mpost_ref[...], RMS_EPS)
    o_ref[...] = _ffn_half_step(x2, pre_ref[...], wg_ref, wu_ref, wd_ref, post_ref[...])


def _const_spec(shape):
    return pl.BlockSpec(shape, lambda *_: (0,) * len(shape), pipeline_mode=pl.Buffered(1))


def _row_spec(tile, cols):
    return pl.BlockSpec((tile, cols), lambda i: (i, 0))


def _ffn_weight_specs():
    return [_const_spec((1, D_MODEL)), _const_spec((D_MODEL, D_FF)), _const_spec((D_MODEL, D_FF)),
            _const_spec((D_FF, D_MODEL)), _const_spec((1, D_MODEL))]


def _row_params():
    return pltpu.CompilerParams(dimension_semantics=("arbitrary",), vmem_limit_bytes=VMEM_LIMIT)


def _ffn_call(x, pre_g, wg, wu, wd, post_g, tile):
    rows = x.shape[0]
    return pl.pallas_call(
        _ffn_kernel,
        grid=(rows // tile,),
        in_specs=[_row_spec(tile, D_MODEL)] + _ffn_weight_specs(),
        out_specs=_row_spec(tile, D_MODEL),
        out_shape=jax.ShapeDtypeStruct((rows, D_MODEL), F32),
        compiler_params=_row_params(),
        name="ffn1",
    )(x, pre_g, wg, wu, wd, post_g)


def _mixout_ffn_call(x1, attn, conv, wo, mpost_g, pre_g, wg, wu, wd, post_g, tile):
    rows = x1.shape[0]
    return pl.pallas_call(
        _mixout_ffn_kernel,
        grid=(rows // tile,),
        in_specs=[_row_spec(tile, D_MODEL), _row_spec(tile, ATTN_W), _row_spec(tile, CONV_W),
                  _const_spec((D_MODEL, D_MODEL)), _const_spec((1, D_MODEL))] + _ffn_weight_specs(),
        out_specs=_row_spec(tile, D_MODEL),
        out_shape=jax.ShapeDtypeStruct((rows, D_MODEL), F32),
        compiler_params=_row_params(),
        name="mixout_ffn2",
    )(x1, attn, conv, wo, mpost_g, pre_g, wg, wu, wd, post_g)


def _in_projection(x, g, w_ref, q_scale):
    h = _rms(x, g, RMS_EPS).astype(BF16)
    p = _dot(h, w_ref[...])
    q = p[:, :ATTN_W] * q_scale
    k = p[:, ATTN_W:2 * ATTN_W]
    v = p[:, 2 * ATTN_W:3 * ATTN_W]
    o = 3 * ATTN_W
    gate_b = p[:, o:o + CONV_W]
    u = p[:, o + CONV_W:o + 2 * CONV_W] * p[:, o + 2 * CONV_W:o + 3 * CONV_W]
    return q, k, v, gate_b, u


def _inproj_main_kernel(tiles_per_seq, x_ref, g_ref, w_ref, cw_ref, init_ref,
                        q_ref, kf_ref, vf_ref, kb_ref, vb_ref, c_ref, tail_ref, ubuf):
    tile = x_ref.shape[0]

    @pl.when(pl.program_id(0) % tiles_per_seq == 0)
    def _():
        ubuf[0:SUBLANES, :] = init_ref[...]

    q, k, v, gate_b, u = _in_projection(x_ref[...], g_ref[...], w_ref, QK_SCALE * LOG2E)
    q_ref[...] = q.astype(BF16)
    kf_ref[...] = k
    vf_ref[...] = v
    kb_ref[...] = k.astype(BF16)
    vb_ref[...] = v.astype(BF16)
    ubuf[SUBLANES:SUBLANES + tile, :] = u
    u1 = ubuf[SUBLANES - 1:SUBLANES - 1 + tile, :]
    u2 = ubuf[SUBLANES - 2:SUBLANES - 2 + tile, :]
    y = cw_ref[0:1, :] * u2 + cw_ref[1:2, :] * u1 + cw_ref[2:3, :] * u
    c_ref[...] = (gate_b * y).astype(BF16)
    tail_ref[...] = u[tile - 2:, :]
    ubuf[0:SUBLANES, :] = u[tile - SUBLANES:, :]


def _inproj_extra_kernel(x_ref, g_ref, w_ref, cw_ref, p1_ref, p2_ref, m1_ref, m2_ref,
                         q_ref, kf_ref, vf_ref, c_ref, u_ref, ubuf):
    tile = x_ref.shape[0]
    q, k, v, gate_b, u = _in_projection(x_ref[...], g_ref[...], w_ref, QK_SCALE)
    q_ref[...] = q
    kf_ref[...] = k
    vf_ref[...] = v
    u_ref[...] = u
    ubuf[0:SUBLANES, :] = jnp.zeros((SUBLANES, CONV_W), F32)
    ubuf[SUBLANES:SUBLANES + tile, :] = u
    u1 = jnp.where(m1_ref[...] > 0.5, p1_ref[...], ubuf[SUBLANES - 1:SUBLANES - 1 + tile, :])
    u2 = jnp.where(m2_ref[...] > 0.5, p2_ref[...], ubuf[SUBLANES - 2:SUBLANES - 2 + tile, :])
    y = cw_ref[0:1, :] * u2 + cw_ref[1:2, :] * u1 + cw_ref[2:3, :] * u
    c_ref[...] = (gate_b * y).astype(BF16)


def _inproj_main_call(x1, g, w_in, conv_w, init, tile, rows_per_seq):
    rows = x1.shape[0]
    tiles_per_seq = rows_per_seq // tile
    n_seq = rows // rows_per_seq
    sd = jax.ShapeDtypeStruct
    return pl.pallas_call(
        functools.partial(_inproj_main_kernel, tiles_per_seq),
        grid=(rows // tile,),
        in_specs=[_row_spec(tile, D_MODEL), _const_spec((1, D_MODEL)), _const_spec((D_MODEL, IN_COLS)),
                  _const_spec((3, CONV_W)), _const_spec((SUBLANES, CONV_W))],
        out_specs=[_row_spec(tile, ATTN_W)] * 5 + [_row_spec(tile, CONV_W),
                   pl.BlockSpec((None, 2, CONV_W), lambda i: (i // tiles_per_seq, 0, 0))],
        out_shape=[sd((rows, ATTN_W), BF16), sd((rows, ATTN_W), F32), sd((rows, ATTN_W), F32),
                   sd((rows, ATTN_W), BF16), sd((rows, ATTN_W), BF16), sd((rows, CONV_W), BF16),
                   sd((n_seq, 2, CONV_W), F32)],
        scratch_shapes=[pltpu.VMEM((SUBLANES + tile, CONV_W), F32)],
        compiler_params=_row_params(),
        name="inproj_main",
    )(x1, g, w_in, conv_w, init)


def _inproj_extra_call(x1, g, w_in, conv_w, p1, p2, m1, m2):
    rows = x1.shape[0]
    sd = jax.ShapeDtypeStruct
    full = lambda cols: pl.BlockSpec((rows, cols), lambda i: (0, 0))
    return pl.pallas_call(
        _inproj_extra_kernel,
        grid=(1,),
        in_specs=[full(D_MODEL), _const_spec((1, D_MODEL)), _const_spec((D_MODEL, IN_COLS)),
                  _const_spec((3, CONV_W))] + [full(CONV_W)] * 4,
        out_specs=[full(ATTN_W)] * 3 + [full(CONV_W)] * 2,
        out_shape=[sd((rows, ATTN_W), F32)] * 3 + [sd((rows, CONV_W), BF16), sd((rows, CONV_W), F32)],
        scratch_shapes=[pltpu.VMEM((SUBLANES + rows, CONV_W), F32)],
        compiler_params=_row_params(),
        name="inproj_extra",
    )(x1, g, w_in, conv_w, p1, p2, m1, m2)


def _lambda(lq1_ref, lk1_ref, lq2_ref, lk2_ref):
    a = jnp.exp(jnp.sum(lq1_ref[...] * lk1_ref[...], axis=-1, keepdims=True))
    b = jnp.exp(jnp.sum(lq2_ref[...] * lk2_ref[...], axis=-1, keepdims=True))
    return a - b + LAMBDA_INIT


def _head_out(o1, o2, lam, g):
    o = o1 - lam * o2
    return _rms(o, g, SUBLN_EPS) * (1.0 - LAMBDA_INIT)


def _stack_maps(q):
    lane = lax.broadcasted_iota(jnp.int32, q.shape, 1)
    zero = jnp.zeros_like(q)
    return jnp.concatenate([jnp.where(lane < HEAD_DIM, q, zero), jnp.where(lane >= HEAD_DIM, q, zero)], axis=0)


def _prompt_attn_kernel(q_ref, k_ref, v_ref, km_ref, vm_ref, lq1_ref, lk1_ref, lq2_ref, lk2_ref, g_ref,
                        o_ref, m_sc, acc_sc):
    tq = q_ref.shape[0]
    i = pl.program_id(1)
    n_chunks = 2 * tq // ROW_CHUNK
    qs = [_stack_maps(q_ref[:, h * V_DIM:(h + 1) * V_DIM]) for h in range(HEADS)]

    def update(h, c, s, v_aug, first):
        rows = slice(c * ROW_CHUNK, (c + 1) * ROW_CHUNK)
        m_cur = jnp.max(s, axis=1, keepdims=True)
        if first:
            m_new = jnp.broadcast_to(m_cur, (ROW_CHUNK, LANES))
            p = jnp.exp2(s - m_cur)
            acc_sc[h, rows, :] = _dot(p.astype(BF16), v_aug)
        else:
            m_prev = m_sc[h, rows, :]
            m_new = jnp.maximum(m_prev, m_cur)
            alpha = jnp.exp2(m_prev - m_new)
            reps = s.shape[1] // LANES
            p = jnp.exp2(s - jnp.concatenate([m_new] * reps, axis=1))
            acc_sc[h, rows, :] = (jnp.concatenate([alpha, alpha], axis=1) * acc_sc[h, rows, :]
                                  + _dot(p.astype(BF16), v_aug))
        m_sc[h, rows, :] = m_new

    def block(k_blk, v_blk, mask_fn, first):
        for h in range(HEADS):
            k = k_blk(h)
            v_aug = jnp.concatenate([v_blk(h), jnp.ones(k.shape, BF16)], axis=1)
            for c in range(n_chunks):
                s = _dot_nt(qs[h][c * ROW_CHUNK:(c + 1) * ROW_CHUNK], k)
                if mask_fn is not None:
                    s = jnp.where(mask_fn(c), s, NEG_INF)
                update(h, c, s, v_aug, first)

    def kv_block(j):
        start = pl.multiple_of(j * KV_TILE, KV_TILE)
        return (lambda h: k_ref[pl.ds(start, KV_TILE), h * V_DIM:(h + 1) * V_DIM],
                lambda h: v_ref[pl.ds(start, KV_TILE), h * V_DIM:(h + 1) * V_DIM])

    k_diag, v_diag = kv_block(i)
    row = lax.broadcasted_iota(jnp.int32, (ROW_CHUNK, KV_TILE + PAGE), 0)
    col = lax.broadcasted_iota(jnp.int32, (ROW_CHUNK, KV_TILE + PAGE), 1)
    block(lambda h: jnp.concatenate([k_diag(h), km_ref[:, h * V_DIM:(h + 1) * V_DIM]], axis=0),
          lambda h: jnp.concatenate([v_diag(h), vm_ref[:, h * V_DIM:(h + 1) * V_DIM]], axis=0),
          lambda c: (col <= row + (c * ROW_CHUNK) % tq) | ((col >= KV_TILE) & (col < KV_TILE + N_META)),
          True)

    def full_block(j, carry):
        block(*kv_block(j), None, False)
        return carry

    lax.fori_loop(0, i, full_block, 0)

    lam = _lambda(lq1_ref, lk1_ref, lq2_ref, lk2_ref)
    for h in range(HEADS):
        acc = acc_sc[h]
        o = acc[:, :V_DIM] / acc[:, V_DIM:]
        o_ref[:, h * V_DIM:(h + 1) * V_DIM] = _head_out(o[:tq], o[tq:], lam, g_ref[...]).astype(o_ref.dtype)


def _prompt_attn_call(q, k, v, km, vm, lq1, lk1, lq2, lk2, subln_g):
    b, s, _ = q.shape
    small = lambda shape: pl.BlockSpec(shape, lambda b_, i: (0,) * len(shape))
    resident = pl.BlockSpec((None, s, ATTN_W), lambda b_, i: (b_, 0, 0), pipeline_mode=pl.Buffered(1))
    return pl.pallas_call(
        _prompt_attn_kernel,
        grid=(b, s // Q_TILE),
        in_specs=[pl.BlockSpec((None, Q_TILE, ATTN_W), lambda b_, i: (b_, i, 0)), resident, resident,
                  small((PAGE, ATTN_W)), small((PAGE, ATTN_W)),
                  small((1, HEAD_DIM)), small((1, HEAD_DIM)), small((1, HEAD_DIM)), small((1, HEAD_DIM)),
                  small((1, V_DIM))],
        out_specs=pl.BlockSpec((None, Q_TILE, ATTN_W), lambda b_, i: (b_, i, 0)),
        out_shape=jax.ShapeDtypeStruct((b, s, ATTN_W), BF16),
        scratch_shapes=[pltpu.VMEM((HEADS, 2 * Q_TILE, LANES), F32),
                        pltpu.VMEM((HEADS, 2 * Q_TILE, 2 * V_DIM), F32)],
        compiler_params=pltpu.CompilerParams(dimension_semantics=("arbitrary",) * 2,
                                             vmem_limit_bytes=VMEM_LIMIT),
        name="prompt_attn",
    )(q, k, v, km, vm, lq1, lk1, lq2, lk2, subln_g)


def _meta_attn_kernel(q_ref, k_ref, v_ref, lq1_ref, lk1_ref, lq2_ref, lk2_ref, g_ref, o_ref):
    t = q_ref.shape[0]
    qs = _stack_maps(q_ref[...].astype(BF16))
    pad = jnp.zeros((PAGE - t, V_DIM), F32)
    k = jnp.concatenate([k_ref[...], pad], axis=0).astype(BF16)
    v = jnp.concatenate([v_ref[...], pad], axis=0).astype(BF16)
    s = _dot_nt(qs, k)
    row = lax.broadcasted_iota(jnp.int32, s.shape, 0)
    col = lax.broadcasted_iota(jnp.int32, s.shape, 1)
    qrow = jnp.where(row >= t, row - t, row)
    s = jnp.where(col <= qrow, s, NEG_INF)
    p = jnp.exp(s - jnp.max(s, axis=-1, keepdims=True))
    o = _dot(p.astype(BF16), v) / jnp.sum(p, axis=-1, keepdims=True)
    lam = _lambda(lq1_ref, lk1_ref, lq2_ref, lk2_ref)
    o_ref[...] = _head_out(o[:t], o[t:], lam, g_ref[...])


def _meta_attn_call(q, k, v, lq1, lk1, lq2, lk2, subln_g):
    small = lambda shape: pl.BlockSpec(shape, lambda h: (0,) * len(shape))
    head = pl.BlockSpec((N_META, V_DIM), lambda h: (0, h))
    return pl.pallas_call(
        _meta_attn_kernel,
        grid=(HEADS,),
        in_specs=[head, head, head, small((1, HEAD_DIM)), small((1, HEAD_DIM)), small((1, HEAD_DIM)),
                  small((1, HEAD_DIM)), small((1, V_DIM))],
        out_specs=head,
        out_shape=jax.ShapeDtypeStruct((N_META, ATTN_W), F32),
        compiler_params=pltpu.CompilerParams(dimension_semantics=("arbitrary",)),
        name="meta_attn",
    )(q, k, v, lq1, lk1, lq2, lk2, subln_g)


def _sample_attn_kernel(n_pages, pt_ref, q_ref, kn_ref, vn_ref, lq1_ref, lk1_ref, lq2_ref, lk2_ref, g_ref,
                        *rest):
    k_refs = rest[:n_pages]
    v_refs = rest[n_pages:2 * n_pages]
    o_ref = rest[2 * n_pages]
    qbd_sc, m_sc, l_sc, acc_sc = rest[2 * n_pages + 1:]
    j = pl.program_id(1)
    t = q_ref.shape[0]
    n_rows = HEADS * 2 * t

    @pl.when(j == 0)
    def _():
        q = q_ref[...]
        qt = jnp.concatenate([q] * (n_rows // t), axis=0)
        row = lax.broadcasted_iota(jnp.int32, qt.shape, 0)
        col = lax.broadcasted_iota(jnp.int32, qt.shape, 1)
        qbd_sc[...] = jnp.where(col // HEAD_DIM == row // t, qt, 0.0).astype(BF16)
        m_sc[...] = jnp.full(m_sc.shape, NEG_INF, F32)
        l_sc[...] = jnp.zeros(l_sc.shape, F32)
        acc_sc[...] = jnp.zeros(acc_sc.shape, F32)

    def accumulate(s, v_heads):
        m_prev = m_sc[...]
        m_new = jnp.maximum(m_prev, jnp.max(s, axis=1, keepdims=True))
        alpha = jnp.exp(m_prev - m_new)
        p = jnp.exp(s - m_new)
        l_sc[...] = alpha * l_sc[...] + jnp.sum(p, axis=1, keepdims=True)
        m_sc[...] = m_new
        pb = p.astype(BF16)
        for h in range(HEADS):
            rs = slice(h * 2 * t, (h + 1) * 2 * t)
            acc_sc[rs, :] = alpha[rs] * acc_sc[rs, :] + _dot(pb[rs], v_heads[h])

    kt = jnp.concatenate([r[...].astype(BF16) for r in k_refs], axis=1)
    v_heads = [jnp.concatenate([r[pl.ds(h, PAGE, stride=HEADS), :].astype(BF16) for r in v_refs], axis=0)
               for h in range(HEADS)]
    accumulate(_dot(qbd_sc[...], kt), v_heads)

    @pl.when(j == pl.num_programs(1) - 1)
    def _():
        pad = jnp.zeros((PAGE - t, ATTN_W), F32)
        kn = jnp.concatenate([kn_ref[...], pad], axis=0).astype(BF16)
        vn = jnp.concatenate([vn_ref[...], pad], axis=0).astype(BF16)
        s = _dot_nt(qbd_sc[...], kn)
        row = lax.broadcasted_iota(jnp.int32, s.shape, 0)
        key = lax.broadcasted_iota(jnp.int32, s.shape, 1)
        s = jnp.where((key < t) & (key <= row % t), s, NEG_INF)
        accumulate(s, [vn[:, h * V_DIM:(h + 1) * V_DIM] for h in range(HEADS)])
        o = acc_sc[...] / l_sc[...]
        lam = _lambda(lq1_ref, lk1_ref, lq2_ref, lk2_ref)
        for h in range(HEADS):
            r0 = h * 2 * t
            o_ref[:, h * V_DIM:(h + 1) * V_DIM] = _head_out(o[r0:r0 + t], o[r0 + t:r0 + 2 * t], lam, g_ref[...])


def _sample_attn_call(page_table, q, k_new, v_new, cache_kt, cache_v, lq1, lk1, lq2, lk2, subln_g, row0, t):
    n_seq, n_pages = page_table.shape
    pps = PAGES_PER_STEP
    steps = n_pages // pps
    blk0 = row0 // t
    n_rows = HEADS * 2 * t
    seq_spec = pl.BlockSpec((t, ATTN_W), lambda b, j, pt: (blk0 + b, 0))
    small = lambda shape: pl.BlockSpec(shape, lambda b, j, pt: (0,) * len(shape))

    def page_spec(r, shape):
        return pl.BlockSpec((None,) + shape, lambda b, j, pt: (pt[b * n_pages + j * pps + r], 0, 0))

    grid_spec = pltpu.PrefetchScalarGridSpec(
        num_scalar_prefetch=1,
        grid=(n_seq, steps),
        in_specs=[seq_spec, seq_spec, seq_spec, small((1, HEAD_DIM)), small((1, HEAD_DIM)),
                  small((1, HEAD_DIM)), small((1, HEAD_DIM)), small((1, V_DIM))]
                 + [page_spec(r, (ATTN_W, PAGE)) for r in range(pps)]
                 + [page_spec(r, (PAGE * HEADS, V_DIM)) for r in range(pps)],
        out_specs=pl.BlockSpec((t, ATTN_W), lambda b, j, pt: (b, 0)),
        scratch_shapes=[pltpu.VMEM((n_rows, ATTN_W), BF16), pltpu.VMEM((n_rows, 1), F32),
                        pltpu.VMEM((n_rows, 1), F32), pltpu.VMEM((n_rows, V_DIM), F32)],
    )
    return pl.pallas_call(
        functools.partial(_sample_attn_kernel, pps),
        grid_spec=grid_spec,
        out_shape=jax.ShapeDtypeStruct((n_seq * t, ATTN_W), F32),
        compiler_params=pltpu.CompilerParams(dimension_semantics=("arbitrary", "arbitrary"),
                                             vmem_limit_bytes=VMEM_LIMIT),
        name="sample_attn",
    )(page_table.reshape(-1), q, k_new, v_new, lq1, lk1, lq2, lk2, subln_g,
      *([cache_kt] * pps), *([cache_v] * pps))


def kernel(x_prompt, x_sample, cache_k, cache_v, state_conv, page_table, meta_tokens, ffn1_pre_g, ffn1_w_gate, ffn1_w_up, ffn1_w_down, ffn1_post_g, mix_pre_g, w_in, lambda_q1, lambda_k1, lambda_q2, lambda_k2, subln_g, conv_w, w_out, mix_post_g, ffn2_pre_g, ffn2_w_gate, ffn2_w_up, ffn2_w_down, ffn2_post_g):
    batch, seq, _ = x_prompt.shape
    n_seq, t_new, _ = x_sample.shape
    n_pool = cache_k.shape[1]
    l = 0

    bf = lambda w: w[l].astype(BF16)
    f1 = (ffn1_pre_g, bf(ffn1_w_gate), bf(ffn1_w_up), bf(ffn1_w_down), ffn1_post_g)
    f2 = (ffn2_pre_g, bf(ffn2_w_gate), bf(ffn2_w_up), bf(ffn2_w_down), ffn2_post_g)
    w_in_b, w_out_b = bf(w_in), bf(w_out)
    lam_args = (lambda_q1, lambda_k1, lambda_q2, lambda_k2, subln_g)
    ckt = jnp.transpose(cache_k[l], (0, 2, 3, 4, 1)).reshape(n_pool, ATTN_W, PAGE)
    cv = cache_v[l].reshape(n_pool, PAGE * HEADS, V_DIM)

    n_s = n_seq * t_new
    n_x = N_META + n_s
    xe = jnp.concatenate([meta_tokens, x_sample.reshape(n_s, D_MODEL)], axis=0)
    st = state_conv[l]
    zpad = lambda a, n: jnp.concatenate([a, jnp.zeros((n_seq, n, CONV_W), F32)], axis=1).reshape(n_s, CONV_W)
    zmeta = jnp.zeros((N_META, CONV_W), F32)
    p1 = jnp.concatenate([zmeta, zpad(st[:, 1:2], t_new - 1)], axis=0)
    p2 = jnp.concatenate([zmeta, zpad(st, t_new - 2)], axis=0)
    r = jnp.arange(n_x)[:, None]
    in_sample = r >= N_META
    pos = jnp.where(in_sample, (r - N_META) % t_new, r)
    m1 = jnp.broadcast_to((pos < 1).astype(F32), (n_x, CONV_W))
    m2 = jnp.broadcast_to((pos < 2).astype(F32), (n_x, CONV_W))

    xe1 = _ffn_call(xe, *f1, tile=n_x)
    qe, ke, ve, ce, ue = _inproj_extra_call(xe1, mix_pre_g, w_in_b, conv_w[l], p1, p2, m1, m2)
    ae_meta = _meta_attn_call(qe[:N_META], ke[:N_META], ve[:N_META], *lam_args)
    ae_s = _sample_attn_call(page_table, qe, ke, ve, ckt, cv, *lam_args, row0=N_META, t=t_new)
    ae = jnp.concatenate([ae_meta, ae_s], axis=0)
    ye = _mixout_ffn_call(xe1, ae, ce, w_out_b, mix_post_g, *f2, tile=n_x)

    rows = batch * seq
    xp = x_prompt.reshape(rows, D_MODEL)
    xp1 = _ffn_call(xp, *f1, tile=ROW_TILE)
    init = jnp.concatenate([jnp.zeros((SUBLANES - 2, CONV_W), F32), ue[N_META - 2:N_META]], axis=0)
    qp, kpf, vpf, kpb, vpb, cp, tail = _inproj_main_call(xp1, mix_pre_g, w_in_b, conv_w[l], init,
                                                        tile=ROW_TILE, rows_per_seq=seq)
    meta_pad = lambda a: jnp.concatenate([a[:N_META], jnp.zeros((PAGE - N_META, ATTN_W), F32)], axis=0).astype(BF16)
    shp = (batch, seq, ATTN_W)
    ap = _prompt_attn_call(qp.reshape(shp), kpb.reshape(shp), vpb.reshape(shp), meta_pad(ke), meta_pad(ve),
                           *lam_args)
    yp = _mixout_ffn_call(xp1, ap.reshape(rows, ATTN_W), cp, w_out_b, mix_post_g, *f2, tile=ROW_TILE)

    y_prompt = yp.reshape(batch, seq, D_MODEL)
    y_sample = ye[N_META:].reshape(n_seq, t_new, D_MODEL)

    def with_meta(meta_rows, main):
        m = jnp.broadcast_to(meta_rows[None], (batch, N_META, ATTN_W))
        return jnp.concatenate([m, main.reshape(batch, seq, ATTN_W)], axis=1)

    k_prompt_new = with_meta(ke[:N_META], kpf).reshape(1, batch, N_META + seq, HEADS, 2, HEAD_DIM)
    v_prompt_new = with_meta(ve[:N_META], vpf).reshape(1, batch, N_META + seq, HEADS, V_DIM)
    conv_prompt_new = tail[None]
    k_sample_new = ke[N_META:].reshape(1, n_seq, t_new, HEADS, 2, HEAD_DIM)
    v_sample_new = ve[N_META:].reshape(1, n_seq, t_new, HEADS, V_DIM)
    conv_sample_new = ue[N_META:].reshape(n_seq, t_new, CONV_W)[None, :, t_new - 2:, :]
    return (y_prompt, y_sample, k_prompt_new, v_prompt_new, conv_prompt_new,
            k_sample_new, v_sample_new, conv_sample_new)
```

```python
import functools
import math

import jax
import jax.numpy as jnp
from jax import lax
from jax.experimental import pallas as pl
from jax.experimental.pallas import tpu as pltpu

F32 = jnp.float32
BF16 = jnp.bfloat16

D_MODEL = 1024
D_FF = 2816
N_META = 16
HEADS = 4
HEAD_DIM = 64
V_DIM = 2 * HEAD_DIM
ATTN_W = HEADS * V_DIM
CONV_W = D_MODEL - ATTN_W
IN_COLS = 3 * ATTN_W + 3 * CONV_W
PAGE = 128
RMS_EPS = 1e-6
SUBLN_EPS = 1e-5
NEG_INF = -1e30
LAMBDA_INIT = 0.8 - 0.6 * math.exp(-0.3 * 0)
QK_SCALE = HEAD_DIM ** -0.5
LOG2E = math.log2(math.e)

SUBLANES = 8
LANES = 128
VMEM_LIMIT = 56 * 1024 * 1024

ROW_TILE = 512
MXU_TILE = 256
FF_CHUNKS = ((0, 6 * MXU_TILE), (6 * MXU_TILE, D_FF))
Q_TILE = 256
KV_TILE = 256
KV_UNROLL = 4
ROW_CHUNK = 128
PAGES_PER_STEP = 16


def _rms(x, g, eps):
    ms = jnp.mean(x * x, axis=-1, keepdims=True)
    return x * lax.rsqrt(ms + eps) * g


def _dot(a, b):
    return jnp.dot(a, b, preferred_element_type=F32)


def _dot_nt(a, b):
    return lax.dot_general(a, b, (((1,), (1,)), ((), ())), preferred_element_type=F32)


def _ffn_half_step(x, pre_g, wg_ref, wu_ref, wd_ref, post_g):
    h = _rms(x, pre_g, RMS_EPS).astype(BF16)
    acc = None
    for lo, hi in FF_CHUNKS:
        sl = slice(lo, hi)
        g = _dot(h, wg_ref[:, sl])
        u = _dot(h, wu_ref[:, sl])
        a = (g * jax.nn.sigmoid(g) * u).astype(BF16)
        d = _dot(a, wd_ref[sl, :])
        acc = d if acc is None else acc + d
    return x + 0.5 * _rms(acc, post_g, RMS_EPS)


def _ffn_kernel(x_ref, pre_ref, wg_ref, wu_ref, wd_ref, post_ref, o_ref):
    o_ref[...] = _ffn_half_step(x_ref[...], pre_ref[...], wg_ref, wu_ref, wd_ref, post_ref[...])


def _mixout_ffn_kernel(x_ref, a_ref, c_ref, wo_ref, mpost_ref,
                       pre_ref, wg_ref, wu_ref, wd_ref, post_ref, o_ref):
    m = _dot(a_ref[...].astype(BF16), wo_ref[:ATTN_W, :]) + _dot(c_ref[...], wo_ref[ATTN_W:, :])
    x2 = x_ref[...] + _rms(m, mpost_ref[...], RMS_EPS)
    o_ref[...] = _ffn_half_step(x2, pre_ref[...], wg_ref, wu_ref, wd_ref, post_ref[...])


def _const_spec(shape):
    return pl.BlockSpec(shape, lambda *_: (0,) * len(shape), pipeline_mode=pl.Buffered(1))


def _row_spec(tile, cols):
    return pl.BlockSpec((tile, cols), lambda i: (i, 0))


def _ffn_weight_specs():
    return [_const_spec((1, D_MODEL)), _const_spec((D_MODEL, D_FF)), _const_spec((D_MODEL, D_FF)),
            _const_spec((D_FF, D_MODEL)), _const_spec((1, D_MODEL))]


def _row_params():
    return pltpu.CompilerParams(dimension_semantics=("arbitrary",), vmem_limit_bytes=VMEM_LIMIT)


def _ffn_call(x, pre_g, wg, wu, wd, post_g, tile):
    rows = x.shape[0]
    return pl.pallas_call(
        _ffn_kernel,
        grid=(rows // tile,),
        in_specs=[_row_spec(tile, D_MODEL)] + _ffn_weight_specs(),
        out_specs=_row_spec(tile, D_MODEL),
        out_shape=jax.ShapeDtypeStruct((rows, D_MODEL), F32),
        compiler_params=_row_params(),
        name="ffn1",
    )(x, pre_g, wg, wu, wd, post_g)


def _mixout_ffn_call(x1, attn, conv, wo, mpost_g, pre_g, wg, wu, wd, post_g, tile):
    rows = x1.shape[0]
    return pl.pallas_call(
        _mixout_ffn_kernel,
        grid=(rows // tile,),
        in_specs=[_row_spec(tile, D_MODEL), _row_spec(tile, ATTN_W), _row_spec(tile, CONV_W),
                  _const_spec((D_MODEL, D_MODEL)), _const_spec((1, D_MODEL))] + _ffn_weight_specs(),
        out_specs=_row_spec(tile, D_MODEL),
        out_shape=jax.ShapeDtypeStruct((rows, D_MODEL), F32),
        compiler_params=_row_params(),
        name="mixout_ffn2",
    )(x1, attn, conv, wo, mpost_g, pre_g, wg, wu, wd, post_g)


def _in_projection(x, g, w_ref, q_scale):
    h = _rms(x, g, RMS_EPS).astype(BF16)
    p = _dot(h, w_ref[...])
    q = p[:, :ATTN_W] * q_scale
    k = p[:, ATTN_W:2 * ATTN_W]
    v = p[:, 2 * ATTN_W:3 * ATTN_W]
    o = 3 * ATTN_W
    gate_b = p[:, o:o + CONV_W]
    u = p[:, o + CONV_W:o + 2 * CONV_W] * p[:, o + 2 * CONV_W:o + 3 * CONV_W]
    return q, k, v, gate_b, u


def _inproj_main_kernel(tiles_per_seq, x_ref, g_ref, w_ref, cw_ref, init_ref,
                        q_ref, kt_ref, v4_ref, kb_ref, vb_ref, c_ref, tail_ref, ubuf):
    tile = x_ref.shape[0]

    @pl.when(pl.program_id(0) % tiles_per_seq == 0)
    def _():
        ubuf[0:SUBLANES, :] = init_ref[...]

    q, k, v, gate_b, u = _in_projection(x_ref[...], g_ref[...], w_ref, QK_SCALE * LOG2E)
    q_ref[...] = q.astype(BF16)
    kt_ref[...] = k.T
    for h in range(HEADS):
        v4_ref[pl.ds(h, tile, stride=HEADS), :] = v[:, h * V_DIM:(h + 1) * V_DIM]
    kb_ref[...] = k.astype(BF16)
    vb_ref[...] = v.astype(BF16)
    ubuf[SUBLANES:SUBLANES + tile, :] = u
    u1 = ubuf[SUBLANES - 1:SUBLANES - 1 + tile, :]
    u2 = ubuf[SUBLANES - 2:SUBLANES - 2 + tile, :]
    y = cw_ref[0:1, :] * u2 + cw_ref[1:2, :] * u1 + cw_ref[2:3, :] * u
    c_ref[...] = (gate_b * y).astype(BF16)
    tail_ref[...] = u[tile - 2:, :]
    ubuf[0:SUBLANES, :] = u[tile - SUBLANES:, :]


def _inproj_extra_kernel(x_ref, g_ref, w_ref, cw_ref, p1_ref, p2_ref, m1_ref, m2_ref,
                         q_ref, kf_ref, vf_ref, c_ref, u_ref, ubuf):
    tile = x_ref.shape[0]
    q, k, v, gate_b, u = _in_projection(x_ref[...], g_ref[...], w_ref, QK_SCALE)
    q_ref[...] = q
    kf_ref[...] = k
    vf_ref[...] = v
    u_ref[...] = u
    ubuf[0:SUBLANES, :] = jnp.zeros((SUBLANES, CONV_W), F32)
    ubuf[SUBLANES:SUBLANES + tile, :] = u
    u1 = jnp.where(m1_ref[...] > 0.5, p1_ref[...], ubuf[SUBLANES - 1:SUBLANES - 1 + tile, :])
    u2 = jnp.where(m2_ref[...] > 0.5, p2_ref[...], ubuf[SUBLANES - 2:SUBLANES - 2 + tile, :])
    y = cw_ref[0:1, :] * u2 + cw_ref[1:2, :] * u1 + cw_ref[2:3, :] * u
    c_ref[...] = (gate_b * y).astype(BF16)


def _inproj_main_call(x1, g, w_in, conv_w, init, tile, rows_per_seq):
    rows = x1.shape[0]
    tiles_per_seq = rows_per_seq // tile
    n_seq = rows // rows_per_seq
    sd = jax.ShapeDtypeStruct
    return pl.pallas_call(
        functools.partial(_inproj_main_kernel, tiles_per_seq),
        grid=(rows // tile,),
        in_specs=[_row_spec(tile, D_MODEL), _const_spec((1, D_MODEL)), _const_spec((D_MODEL, IN_COLS)),
                  _const_spec((3, CONV_W)), _const_spec((SUBLANES, CONV_W))],
        out_specs=[_row_spec(tile, ATTN_W),
                   pl.BlockSpec((None, ATTN_W, tile), lambda i: (i // tiles_per_seq, 0, i % tiles_per_seq)),
                   _row_spec(tile * HEADS, V_DIM),
                   _row_spec(tile, ATTN_W), _row_spec(tile, ATTN_W), _row_spec(tile, CONV_W),
                   pl.BlockSpec((None, 2, CONV_W), lambda i: (i // tiles_per_seq, 0, 0))],
        out_shape=[sd((rows, ATTN_W), BF16), sd((n_seq, ATTN_W, rows_per_seq), F32),
                   sd((rows * HEADS, V_DIM), F32),
                   sd((rows, ATTN_W), BF16), sd((rows, ATTN_W), BF16), sd((rows, CONV_W), BF16),
                   sd((n_seq, 2, CONV_W), F32)],
        scratch_shapes=[pltpu.VMEM((SUBLANES + tile, CONV_W), F32)],
        compiler_params=_row_params(),
        name="inproj_main",
    )(x1, g, w_in, conv_w, init)


def _inproj_extra_call(x1, g, w_in, conv_w, p1, p2, m1, m2):
    rows = x1.shape[0]
    sd = jax.ShapeDtypeStruct
    full = lambda cols: pl.BlockSpec((rows, cols), lambda i: (0, 0))
    return pl.pallas_call(
        _inproj_extra_kernel,
        grid=(1,),
        in_specs=[full(D_MODEL), _const_spec((1, D_MODEL)), _const_spec((D_MODEL, IN_COLS)),
                  _const_spec((3, CONV_W))] + [full(CONV_W)] * 4,
        out_specs=[full(ATTN_W)] * 3 + [full(CONV_W)] * 2,
        out_shape=[sd((rows, ATTN_W), F32)] * 3 + [sd((rows, CONV_W), BF16), sd((rows, CONV_W), F32)],
        scratch_shapes=[pltpu.VMEM((SUBLANES + rows, CONV_W), F32)],
        compiler_params=_row_params(),
        name="inproj_extra",
    )(x1, g, w_in, conv_w, p1, p2, m1, m2)


def _lambda(lq1_ref, lk1_ref, lq2_ref, lk2_ref):
    a = jnp.exp(jnp.sum(lq1_ref[...] * lk1_ref[...], axis=-1, keepdims=True))
    b = jnp.exp(jnp.sum(lq2_ref[...] * lk2_ref[...], axis=-1, keepdims=True))
    return a - b + LAMBDA_INIT


def _head_out(o1, o2, lam, g):
    o = o1 - lam * o2
    return _rms(o, g, SUBLN_EPS) * (1.0 - LAMBDA_INIT)


def _stack_maps(q):
    lane = lax.broadcasted_iota(jnp.int32, q.shape, 1)
    zero = jnp.zeros_like(q)
    return jnp.concatenate([jnp.where(lane < HEAD_DIM, q, zero), jnp.where(lane >= HEAD_DIM, q, zero)], axis=0)


def _prompt_attn_kernel(q_ref, k_ref, v_ref, km_ref, vm_ref, lq1_ref, lk1_ref, lq2_ref, lk2_ref, g_ref,
                        o_ref, m_sc, acc_sc):
    tq = q_ref.shape[0]
    i = pl.program_id(1)
    n_chunks = 2 * tq // ROW_CHUNK
    qs = [_stack_maps(q_ref[:, h * V_DIM:(h + 1) * V_DIM]) for h in range(HEADS)]

    def update(h, c, s, v_aug, first):
        rows = slice(c * ROW_CHUNK, (c + 1) * ROW_CHUNK)
        m_cur = jnp.max(s, axis=1, keepdims=True)
        if first:
            m_new = jnp.broadcast_to(m_cur, (ROW_CHUNK, LANES))
            p = jnp.exp2(s - m_cur)
            acc_sc[h, rows, :] = _dot(p.astype(BF16), v_aug)
        else:
            m_prev = m_sc[h, rows, :]
            m_new = jnp.maximum(m_prev, m_cur)
            alpha = jnp.exp2(m_prev - m_new)
            reps = s.shape[1] // LANES
            p = jnp.exp2(s - jnp.concatenate([m_new] * reps, axis=1))
            acc_sc[h, rows, :] = (jnp.concatenate([alpha, alpha], axis=1) * acc_sc[h, rows, :]
                                  + _dot(p.astype(BF16), v_aug))
        m_sc[h, rows, :] = m_new

    def block(k_blk, v_blk, mask_fn, first):
        for h in range(HEADS):
            k = k_blk(h)
            v_aug = jnp.concatenate([v_blk(h), jnp.ones(k.shape, BF16)], axis=1)
            for c in range(n_chunks):
                s = _dot_nt(qs[h][c * ROW_CHUNK:(c + 1) * ROW_CHUNK], k)
                if mask_fn is not None:
                    s = jnp.where(mask_fn(c), s, NEG_INF)
                update(h, c, s, v_aug, first)

    def kv_block(j):
        start = pl.multiple_of(j * KV_TILE, KV_TILE)
        return (lambda h: k_ref[pl.ds(start, KV_TILE), h * V_DIM:(h + 1) * V_DIM],
                lambda h: v_ref[pl.ds(start, KV_TILE), h * V_DIM:(h + 1) * V_DIM])

    k_diag, v_diag = kv_block(i)
    row = lax.broadcasted_iota(jnp.int32, (ROW_CHUNK, KV_TILE + PAGE), 0)
    col = lax.broadcasted_iota(jnp.int32, (ROW_CHUNK, KV_TILE + PAGE), 1)
    block(lambda h: jnp.concatenate([k_diag(h), km_ref[:, h * V_DIM:(h + 1) * V_DIM]], axis=0),
          lambda h: jnp.concatenate([v_diag(h), vm_ref[:, h * V_DIM:(h + 1) * V_DIM]], axis=0),
          lambda c: (col <= row + (c * ROW_CHUNK) % tq) | ((col >= KV_TILE) & (col < KV_TILE + N_META)),
          True)

    rem = i % KV_UNROLL
    width = 1
    while width < KV_UNROLL:
        def leftover(width=width):
            first_blk = (i // KV_UNROLL) * KV_UNROLL + (rem & (width - 1))
            for d in range(width):
                block(*kv_block(first_blk + d), None, False)
        pl.when(rem & width != 0)(leftover)
        width *= 2

    def full_group(jj, carry):
        for d in range(KV_UNROLL):
            block(*kv_block(KV_UNROLL * jj + d), None, False)
        return carry

    lax.fori_loop(0, i // KV_UNROLL, full_group, 0)

    lam = _lambda(lq1_ref, lk1_ref, lq2_ref, lk2_ref)
    for h in range(HEADS):
        acc = acc_sc[h]
        o = acc[:, :V_DIM] / acc[:, V_DIM:]
        o_ref[:, h * V_DIM:(h + 1) * V_DIM] = _head_out(o[:tq], o[tq:], lam, g_ref[...]).astype(o_ref.dtype)


def _prompt_attn_call(q, k, v, km, vm, lq1, lk1, lq2, lk2, subln_g):
    b, s, _ = q.shape
    small = lambda shape: pl.BlockSpec(shape, lambda b_, i: (0,) * len(shape))
    resident = pl.BlockSpec((None, s, ATTN_W), lambda b_, i: (b_, 0, 0), pipeline_mode=pl.Buffered(1))
    return pl.pallas_call(
        _prompt_attn_kernel,
        grid=(b, s // Q_TILE),
        in_specs=[pl.BlockSpec((None, Q_TILE, ATTN_W), lambda b_, i: (b_, i, 0)), resident, resident,
                  small((PAGE, ATTN_W)), small((PAGE, ATTN_W)),
                  small((1, HEAD_DIM)), small((1, HEAD_DIM)), small((1, HEAD_DIM)), small((1, HEAD_DIM)),
                  small((1, V_DIM))],
        out_specs=pl.BlockSpec((None, Q_TILE, ATTN_W), lambda b_, i: (b_, i, 0)),
        out_shape=jax.ShapeDtypeStruct((b, s, ATTN_W), BF16),
        scratch_shapes=[pltpu.VMEM((HEADS, 2 * Q_TILE, LANES), F32),
                        pltpu.VMEM((HEADS, 2 * Q_TILE, 2 * V_DIM), F32)],
        compiler_params=pltpu.CompilerParams(dimension_semantics=("arbitrary",) * 2,
                                             vmem_limit_bytes=VMEM_LIMIT),
        name="prompt_attn",
    )(q, k, v, km, vm, lq1, lk1, lq2, lk2, subln_g)


def _meta_attn_kernel(q_ref, k_ref, v_ref, lq1_ref, lk1_ref, lq2_ref, lk2_ref, g_ref, o_ref):
    t = q_ref.shape[0]
    qs = _stack_maps(q_ref[...].astype(BF16))
    pad = jnp.zeros((PAGE - t, V_DIM), F32)
    k = jnp.concatenate([k_ref[...], pad], axis=0).astype(BF16)
    v = jnp.concatenate([v_ref[...], pad], axis=0).astype(BF16)
    s = _dot_nt(qs, k)
    row = lax.broadcasted_iota(jnp.int32, s.shape, 0)
    col = lax.broadcasted_iota(jnp.int32, s.shape, 1)
    qrow = jnp.where(row >= t, row - t, row)
    s = jnp.where(col <= qrow, s, NEG_INF)
    p = jnp.exp(s - jnp.max(s, axis=-1, keepdims=True))
    o = _dot(p.astype(BF16), v) / jnp.sum(p, axis=-1, keepdims=True)
    lam = _lambda(lq1_ref, lk1_ref, lq2_ref, lk2_ref)
    o_ref[...] = _head_out(o[:t], o[t:], lam, g_ref[...])


def _meta_attn_call(q, k, v, lq1, lk1, lq2, lk2, subln_g):
    small = lambda shape: pl.BlockSpec(shape, lambda h: (0,) * len(shape))
    head = pl.BlockSpec((N_META, V_DIM), lambda h: (0, h))
    return pl.pallas_call(
        _meta_attn_kernel,
        grid=(HEADS,),
        in_specs=[head, head, head, small((1, HEAD_DIM)), small((1, HEAD_DIM)), small((1, HEAD_DIM)),
                  small((1, HEAD_DIM)), small((1, V_DIM))],
        out_specs=head,
        out_shape=jax.ShapeDtypeStruct((N_META, ATTN_W), F32),
        compiler_params=pltpu.CompilerParams(dimension_semantics=("arbitrary",)),
        name="meta_attn",
    )(q, k, v, lq1, lk1, lq2, lk2, subln_g)


def _sample_attn_kernel(n_pages, pt_ref, q_ref, kn_ref, vn_ref, lq1_ref, lk1_ref, lq2_ref, lk2_ref, g_ref,
                        *rest):
    k_refs = rest[:n_pages]
    v_refs = rest[n_pages:2 * n_pages]
    o_ref = rest[2 * n_pages]
    qbd_sc, m_sc, l_sc, acc_sc = rest[2 * n_pages + 1:]
    j = pl.program_id(1)
    t = q_ref.shape[0]
    n_rows = HEADS * 2 * t

    @pl.when(j == 0)
    def _():
        q = q_ref[...]
        qt = jnp.concatenate([q] * (n_rows // t), axis=0)
        row = lax.broadcasted_iota(jnp.int32, qt.shape, 0)
        col = lax.broadcasted_iota(jnp.int32, qt.shape, 1)
        qbd_sc[...] = jnp.where(col // HEAD_DIM == row // t, qt, 0.0).astype(BF16)
        m_sc[...] = jnp.full(m_sc.shape, NEG_INF, F32)
        l_sc[...] = jnp.zeros(l_sc.shape, F32)
        acc_sc[...] = jnp.zeros(acc_sc.shape, F32)

    def accumulate(s, v_heads):
        m_prev = m_sc[...]
        m_new = jnp.maximum(m_prev, jnp.max(s, axis=1, keepdims=True))
        alpha = jnp.exp(m_prev - m_new)
        p = jnp.exp(s - m_new)
        l_sc[...] = alpha * l_sc[...] + jnp.sum(p, axis=1, keepdims=True)
        m_sc[...] = m_new
        pb = p.astype(BF16)
        for h in range(HEADS):
            rs = slice(h * 2 * t, (h + 1) * 2 * t)
            acc_sc[rs, :] = alpha[rs] * acc_sc[rs, :] + _dot(pb[rs], v_heads[h])

    kt = jnp.concatenate([r[...].astype(BF16) for r in k_refs], axis=1)
    v_heads = [jnp.concatenate([r[pl.ds(h, PAGE, stride=HEADS), :].astype(BF16) for r in v_refs], axis=0)
               for h in range(HEADS)]
    accumulate(_dot(qbd_sc[...], kt), v_heads)

    @pl.when(j == pl.num_programs(1) - 1)
    def _():
        pad = jnp.zeros((PAGE - t, ATTN_W), F32)
        kn = jnp.concatenate([kn_ref[...], pad], axis=0).astype(BF16)
        vn = jnp.concatenate([vn_ref[...], pad], axis=0).astype(BF16)
        s = _dot_nt(qbd_sc[...], kn)
        row = lax.broadcasted_iota(jnp.int32, s.shape, 0)
        key = lax.broadcasted_iota(jnp.int32, s.shape, 1)
        s = jnp.where((key < t) & (key <= row % t), s, NEG_INF)
        accumulate(s, [vn[:, h * V_DIM:(h + 1) * V_DIM] for h in range(HEADS)])
        o = acc_sc[...] / l_sc[...]
        lam = _lambda(lq1_ref, lk1_ref, lq2_ref, lk2_ref)
        for h in range(HEADS):
            r0 = h * 2 * t
            o_ref[:, h * V_DIM:(h + 1) * V_DIM] = _head_out(o[r0:r0 + t], o[r0 + t:r0 + 2 * t], lam, g_ref[...])


def _sample_attn_call(page_table, q, k_new, v_new, cache_kt, cache_v, lq1, lk1, lq2, lk2, subln_g, row0, t):
    n_seq, n_pages = page_table.shape
    pps = PAGES_PER_STEP
    steps = n_pages // pps
    blk0 = row0 // t
    n_rows = HEADS * 2 * t
    seq_spec = pl.BlockSpec((t, ATTN_W), lambda b, j, pt: (blk0 + b, 0))
    small = lambda shape: pl.BlockSpec(shape, lambda b, j, pt: (0,) * len(shape))

    def page_spec(r, shape):
        return pl.BlockSpec((None,) + shape, lambda b, j, pt: (pt[b * n_pages + j * pps + r], 0, 0))

    grid_spec = pltpu.PrefetchScalarGridSpec(
        num_scalar_prefetch=1,
        grid=(n_seq, steps),
        in_specs=[seq_spec, seq_spec, seq_spec, small((1, HEAD_DIM)), small((1, HEAD_DIM)),
                  small((1, HEAD_DIM)), small((1, HEAD_DIM)), small((1, V_DIM))]
                 + [page_spec(r, (ATTN_W, PAGE)) for r in range(pps)]
                 + [page_spec(r, (PAGE * HEADS, V_DIM)) for r in range(pps)],
        out_specs=pl.BlockSpec((t, ATTN_W), lambda b, j, pt: (b, 0)),
        scratch_shapes=[pltpu.VMEM((n_rows, ATTN_W), BF16), pltpu.VMEM((n_rows, 1), F32),
                        pltpu.VMEM((n_rows, 1), F32), pltpu.VMEM((n_rows, V_DIM), F32)],
    )
    return pl.pallas_call(
        functools.partial(_sample_attn_kernel, pps),
        grid_spec=grid_spec,
        out_shape=jax.ShapeDtypeStruct((n_seq * t, ATTN_W), F32),
        compiler_params=pltpu.CompilerParams(dimension_semantics=("arbitrary", "arbitrary"),
                                             vmem_limit_bytes=VMEM_LIMIT),
        name="sample_attn",
    )(page_table.reshape(-1), q, k_new, v_new, lq1, lk1, lq2, lk2, subln_g,
      *([cache_kt] * pps), *([cache_v] * pps))


def kernel(x_prompt, x_sample, cache_k, cache_v, state_conv, page_table, meta_tokens, ffn1_pre_g, ffn1_w_gate, ffn1_w_up, ffn1_w_down, ffn1_post_g, mix_pre_g, w_in, lambda_q1, lambda_k1, lambda_q2, lambda_k2, subln_g, conv_w, w_out, mix_post_g, ffn2_pre_g, ffn2_w_gate, ffn2_w_up, ffn2_w_down, ffn2_post_g):
    batch, seq, _ = x_prompt.shape
    n_seq, t_new, _ = x_sample.shape
    n_pool = cache_k.shape[1]
    l = 0

    bf = lambda w: w[l].astype(BF16)
    f1 = (ffn1_pre_g, bf(ffn1_w_gate), bf(ffn1_w_up), bf(ffn1_w_down), ffn1_post_g)
    f2 = (ffn2_pre_g, bf(ffn2_w_gate), bf(ffn2_w_up), bf(ffn2_w_down), ffn2_post_g)
    w_in_b, w_out_b = bf(w_in), bf(w_out)
    lam_args = (lambda_q1, lambda_k1, lambda_q2, lambda_k2, subln_g)
    ckt = jnp.transpose(cache_k[l], (0, 2, 3, 4, 1)).reshape(n_pool, ATTN_W, PAGE)
    cv = cache_v[l].reshape(n_pool, PAGE * HEADS, V_DIM)

    n_s = n_seq * t_new
    n_x = N_META + n_s
    xe = jnp.concatenate([meta_tokens, x_sample.reshape(n_s, D_MODEL)], axis=0)
    st = state_conv[l]
    zpad = lambda a, n: jnp.concatenate([a, jnp.zeros((n_seq, n, CONV_W), F32)], axis=1).reshape(n_s, CONV_W)
    zmeta = jnp.zeros((N_META, CONV_W), F32)
    p1 = jnp.concatenate([zmeta, zpad(st[:, 1:2], t_new - 1)], axis=0)
    p2 = jnp.concatenate([zmeta, zpad(st, t_new - 2)], axis=0)
    r = jnp.arange(n_x)[:, None]
    in_sample = r >= N_META
    pos = jnp.where(in_sample, (r - N_META) % t_new, r)
    m1 = jnp.broadcast_to((pos < 1).astype(F32), (n_x, CONV_W))
    m2 = jnp.broadcast_to((pos < 2).astype(F32), (n_x, CONV_W))

    xe1 = _ffn_call(xe, *f1, tile=n_x)
    qe, ke, ve, ce, ue = _inproj_extra_call(xe1, mix_pre_g, w_in_b, conv_w[l], p1, p2, m1, m2)
    ae_meta = _meta_attn_call(qe[:N_META], ke[:N_META], ve[:N_META], *lam_args)
    ae_s = _sample_attn_call(page_table, qe, ke, ve, ckt, cv, *lam_args, row0=N_META, t=t_new)
    ae = jnp.concatenate([ae_meta, ae_s], axis=0)
    ye = _mixout_ffn_call(xe1, ae, ce, w_out_b, mix_post_g, *f2, tile=n_x)

    rows = batch * seq
    xp = x_prompt.reshape(rows, D_MODEL)
    xp1 = _ffn_call(xp, *f1, tile=ROW_TILE)
    init = jnp.concatenate([jnp.zeros((SUBLANES - 2, CONV_W), F32), ue[N_META - 2:N_META]], axis=0)
    qp, kpt, vp4, kpb, vpb, cp, tail = _inproj_main_call(xp1, mix_pre_g, w_in_b, conv_w[l], init,
                                                        tile=ROW_TILE, rows_per_seq=seq)
    meta_pad = lambda a: jnp.concatenate([a[:N_META], jnp.zeros((PAGE - N_META, ATTN_W), F32)], axis=0).astype(BF16)
    shp = (batch, seq, ATTN_W)
    ap = _prompt_attn_call(qp.reshape(shp), kpb.reshape(shp), vpb.reshape(shp), meta_pad(ke), meta_pad(ve),
                           *lam_args)
    yp = _mixout_ffn_call(xp1, ap.reshape(rows, ATTN_W), cp, w_out_b, mix_post_g, *f2, tile=ROW_TILE)

    y_prompt = yp.reshape(batch, seq, D_MODEL)
    y_sample = ye[N_META:].reshape(n_seq, t_new, D_MODEL)

    kt_meta = jnp.broadcast_to(ke[:N_META].T[None], (batch, ATTN_W, N_META))
    kt_all = jnp.concatenate([kt_meta, kpt], axis=2).reshape(batch, HEADS, 2, HEAD_DIM, N_META + seq)
    k_prompt_new = jnp.transpose(kt_all, (0, 4, 1, 2, 3))[None]
    v_meta = jnp.broadcast_to(ve[:N_META].reshape(1, N_META, HEADS, V_DIM), (batch, N_META, HEADS, V_DIM))
    v_prompt_new = jnp.concatenate([v_meta, vp4.reshape(batch, seq, HEADS, V_DIM)], axis=1)[None]
    conv_prompt_new = tail[None]
    k_sample_new = ke[N_META:].reshape(1, n_seq, t_new, HEADS, 2, HEAD_DIM)
    v_sample_new = ve[N_META:].reshape(1, n_seq, t_new, HEADS, V_DIM)
    conv_sample_new = ue[N_META:].reshape(n_seq, t_new, CONV_W)[None, :, t_new - 2:, :]
    return (y_prompt, y_sample, k_prompt_new, v_prompt_new, conv_prompt_new,
            k_sample_new, v_sample_new, conv_sample_new)
```

```python
import functools
import math

import jax
import jax.numpy as jnp
from jax import lax
from jax.experimental import pallas as pl
from jax.experimental.pallas import tpu as pltpu

F32 = jnp.float32
BF16 = jnp.bfloat16

D_MODEL = 1024
D_FF = 2816
N_META = 16
HEADS = 4
HEAD_DIM = 64
V_DIM = 2 * HEAD_DIM
ATTN_W = HEADS * V_DIM
CONV_W = D_MODEL - ATTN_W
IN_COLS = 3 * ATTN_W + 3 * CONV_W
PAGE = 128
RMS_EPS = 1e-6
SUBLN_EPS = 1e-5
NEG_INF = -1e30
LAMBDA_INIT = 0.8 - 0.6 * math.exp(-0.3 * 0)
QK_SCALE = HEAD_DIM ** -0.5
LOG2E = math.log2(math.e)

SUBLANES = 8
LANES = 128
VMEM_LIMIT = 56 * 1024 * 1024

ROW_TILE = 512
HOST_ROW_TILE = 256
MXU_TILE = 256
FF_CHUNKS = ((0, 6 * MXU_TILE), (6 * MXU_TILE, D_FF))
Q_TILE = 256
KV_TILE = 256
KV_UNROLL = 8
ROW_CHUNK = 512
PAGES_PER_STEP = 16


def _rms(x, g, eps):
    ms = jnp.mean(x * x, axis=-1, keepdims=True)
    return x * lax.rsqrt(ms + eps) * g


def _dot(a, b):
    return jnp.dot(a, b, preferred_element_type=F32)


def _dot_nt(a, b):
    return lax.dot_general(a, b, (((1,), (1,)), ((), ())), preferred_element_type=F32)


def _ffn_half_step(x, pre_g, wg_ref, wu_ref, wd_ref, post_g):
    h = _rms(x, pre_g, RMS_EPS).astype(BF16)
    acc = None
    for lo, hi in FF_CHUNKS:
        sl = slice(lo, hi)
        g = _dot(h, wg_ref[:, sl])
        u = _dot(h, wu_ref[:, sl])
        a = (g * jax.nn.sigmoid(g) * u).astype(BF16)
        d = _dot(a, wd_ref[sl, :])
        acc = d if acc is None else acc + d
    return x + 0.5 * _rms(acc, post_g, RMS_EPS)


def _lambda(lq1_ref, lk1_ref, lq2_ref, lk2_ref):
    a = jnp.exp(jnp.sum(lq1_ref[...] * lk1_ref[...], axis=-1, keepdims=True))
    b = jnp.exp(jnp.sum(lq2_ref[...] * lk2_ref[...], axis=-1, keepdims=True))
    return a - b + LAMBDA_INIT


def _head_out(o1, o2, lam, g):
    o = o1 - lam * o2
    return _rms(o, g, SUBLN_EPS) * (1.0 - LAMBDA_INIT)


class _SampleAttn:
    N_REFS = 8

    def __init__(self, in_refs, k_refs, v_refs, o_ref, qbd_sc, m_sc, l_sc, acc_sc):
        self.q_ref, self.kn_ref, self.vn_ref = in_refs[:3]
        self.lam_refs, self.g_ref = in_refs[3:7], in_refs[7]
        self.k_refs, self.v_refs, self.o_ref = k_refs, v_refs, o_ref
        self.qbd_sc, self.m_sc, self.l_sc, self.acc_sc = qbd_sc, m_sc, l_sc, acc_sc
        self.t = self.q_ref.shape[0]

    def _accumulate(self, s, v_pairs):
        t, m_sc, l_sc, acc_sc = self.t, self.m_sc, self.l_sc, self.acc_sc
        m_prev = m_sc[...]
        m_new = jnp.maximum(m_prev, jnp.max(s, axis=1, keepdims=True))
        alpha = jnp.exp(m_prev - m_new)
        p = jnp.exp(s - m_new)
        l_sc[...] = alpha * l_sc[...] + jnp.sum(p, axis=1, keepdims=True)
        m_sc[...] = m_new
        pb = p.astype(BF16)
        for g in range(HEADS // 2):
            pv = _dot(pb[g * 4 * t:(g + 1) * 4 * t], v_pairs[g])
            for e in range(2):
                rs = slice((2 * g + e) * 2 * t, (2 * g + e + 1) * 2 * t)
                acc_sc[rs, :] = (alpha[rs] * acc_sc[rs, :]
                                 + pv[e * 2 * t:(e + 1) * 2 * t, e * V_DIM:(e + 1) * V_DIM])

    def start(self, j):
        t = self.t
        n_rows = HEADS * 2 * t

        @pl.when(j == 0)
        def _():
            qt = jnp.concatenate([self.q_ref[...]] * (n_rows // t), axis=0)
            row = lax.broadcasted_iota(jnp.int32, qt.shape, 0)
            col = lax.broadcasted_iota(jnp.int32, qt.shape, 1)
            self.qbd_sc[...] = jnp.where(col // HEAD_DIM == row // t, qt, 0.0).astype(BF16)
            self.m_sc[...] = jnp.full(self.m_sc.shape, NEG_INF, F32)
            self.l_sc[...] = jnp.zeros(self.l_sc.shape, F32)
            self.acc_sc[...] = jnp.zeros(self.acc_sc.shape, F32)

    def pages(self):
        kt = jnp.concatenate([r[...].astype(BF16) for r in self.k_refs], axis=1)
        head = lambda r, h: r[pl.ds(h, PAGE, stride=HEADS), :].astype(BF16)
        v_pairs = [jnp.concatenate([jnp.concatenate([head(r, 2 * g), head(r, 2 * g + 1)], axis=1)
                                    for r in self.v_refs], axis=0) for g in range(HEADS // 2)]
        self._accumulate(_dot(self.qbd_sc[...], kt), v_pairs)

    def finish(self, j, n_steps):
        t = self.t

        @pl.when(j == n_steps - 1)
        def _():
            pad = jnp.zeros((PAGE - t, ATTN_W), F32)
            kn = jnp.concatenate([self.kn_ref[...], pad], axis=0).astype(BF16)
            vn = jnp.concatenate([self.vn_ref[...], pad], axis=0).astype(BF16)
            s = _dot_nt(self.qbd_sc[...], kn)
            row = lax.broadcasted_iota(jnp.int32, s.shape, 0)
            key = lax.broadcasted_iota(jnp.int32, s.shape, 1)
            s = jnp.where((key < t) & (key <= row % t), s, NEG_INF)
            self._accumulate(s, [vn[:, g * 2 * V_DIM:(g + 1) * 2 * V_DIM] for g in range(HEADS // 2)])
            o = self.acc_sc[...] / self.l_sc[...]
            lam = _lambda(*self.lam_refs)
            for h in range(HEADS):
                r0 = h * 2 * t
                self.o_ref[:, h * V_DIM:(h + 1) * V_DIM] = _head_out(
                    o[r0:r0 + t], o[r0 + t:r0 + 2 * t], lam, self.g_ref[...])


def _ffn_stage_kernel(has_mix, host_pages, host_steps, *refs):
    refs = list(refs)
    take = lambda n: [refs.pop(0) for _ in range(n)]
    if host_pages:
        (pt_ref,) = take(1)
    (x_ref,) = take(1)
    if has_mix:
        a_ref, c_ref, wo_ref, mpost_ref = take(4)
    pre_ref, wg_ref, wu_ref, wd_ref, post_ref = take(5)
    if host_pages:
        sample_refs = take(_SampleAttn.N_REFS)
        k_refs, v_refs = take(host_pages), take(host_pages)
    (o_ref,) = take(1)
    if host_pages:
        j = pl.program_id(0) % host_steps
        sample = _SampleAttn(sample_refs, k_refs, v_refs, *refs)
        sample.start(j)
        sample.pages()
    x = x_ref[...]
    if has_mix:
        m = _dot(a_ref[...].astype(BF16), wo_ref[:ATTN_W, :]) + _dot(c_ref[...], wo_ref[ATTN_W:, :])
        x = x + _rms(m, mpost_ref[...], RMS_EPS)
    o_ref[...] = _ffn_half_step(x, pre_ref[...], wg_ref, wu_ref, wd_ref, post_ref[...])
    if host_pages:
        sample.finish(j, host_steps)


def _const_spec(shape):
    return pl.BlockSpec(shape, lambda *_: (0,) * len(shape), pipeline_mode=pl.Buffered(1))


def _row_spec(tile, cols):
    return pl.BlockSpec((tile, cols), lambda i, *_: (i, 0))


def _ffn_weight_specs():
    return [_const_spec((1, D_MODEL)), _const_spec((D_MODEL, D_FF)), _const_spec((D_MODEL, D_FF)),
            _const_spec((D_FF, D_MODEL)), _const_spec((1, D_MODEL))]


def _row_params():
    return pltpu.CompilerParams(dimension_semantics=("arbitrary",), vmem_limit_bytes=VMEM_LIMIT)


def _ffn_stage_call(x, ffn, tile, mix=None, host=None, name="ffn"):
    rows = x.shape[0]
    steps = rows // tile
    args = [x]
    in_specs = [_row_spec(tile, D_MODEL)]
    if mix is not None:
        args += list(mix)
        in_specs += [_row_spec(tile, ATTN_W), _row_spec(tile, CONV_W),
                     _const_spec((D_MODEL, D_MODEL)), _const_spec((1, D_MODEL))]
    args += list(ffn)
    in_specs += _ffn_weight_specs()
    out_specs = _row_spec(tile, D_MODEL)
    out_shape = jax.ShapeDtypeStruct((rows, D_MODEL), F32)
    if host is None:
        return pl.pallas_call(
            functools.partial(_ffn_stage_kernel, mix is not None, 0, 0),
            grid=(steps,), in_specs=in_specs, out_specs=out_specs, out_shape=out_shape,
            compiler_params=_row_params(), name=name,
        )(*args)

    page_table, seq0, n_host_seq, q, k_new, v_new, row0, t, cache_kt, cache_v, lam_args = host
    n_pages = page_table.shape[1]
    pps = PAGES_PER_STEP
    host_steps = n_pages // pps
    assert steps == n_host_seq * host_steps and row0 % t == 0
    blk0 = row0 // t + seq0
    n_rows = HEADS * 2 * t
    seq_spec = pl.BlockSpec((t, ATTN_W), lambda i, pt: (blk0 + i // host_steps, 0))
    small = lambda shape: pl.BlockSpec(shape, lambda i, pt: (0,) * len(shape))

    def page_spec(r, shape):
        return pl.BlockSpec(
            (None,) + shape,
            lambda i, pt: (pt[(seq0 + i // host_steps) * n_pages + (i % host_steps) * pps + r], 0, 0))

    args += [q, k_new, v_new, *lam_args] + [cache_kt] * pps + [cache_v] * pps
    in_specs += ([seq_spec] * 3 + [small((1, HEAD_DIM))] * 4 + [small((1, V_DIM))]
                 + [page_spec(r, (ATTN_W, PAGE)) for r in range(pps)]
                 + [page_spec(r, (PAGE * HEADS, V_DIM)) for r in range(pps)])
    grid_spec = pltpu.PrefetchScalarGridSpec(
        num_scalar_prefetch=1,
        grid=(steps,),
        in_specs=in_specs,
        out_specs=[out_specs, pl.BlockSpec((t, ATTN_W), lambda i, pt: (i // host_steps, 0))],
        scratch_shapes=[pltpu.VMEM((n_rows, ATTN_W), BF16), pltpu.VMEM((n_rows, 1), F32),
                        pltpu.VMEM((n_rows, 1), F32), pltpu.VMEM((n_rows, V_DIM), F32)],
    )
    return pl.pallas_call(
        functools.partial(_ffn_stage_kernel, mix is not None, pps, host_steps),
        grid_spec=grid_spec,
        out_shape=[out_shape, jax.ShapeDtypeStruct((n_host_seq * t, ATTN_W), F32)],
        compiler_params=_row_params(), name=name,
    )(page_table.reshape(-1), *args)


def _in_projection(x, g, w_ref, q_scale):
    h = _rms(x, g, RMS_EPS).astype(BF16)
    p = _dot(h, w_ref[...])
    q = p[:, :ATTN_W] * q_scale
    k = p[:, ATTN_W:2 * ATTN_W]
    v = p[:, 2 * ATTN_W:3 * ATTN_W]
    o = 3 * ATTN_W
    gate_b = p[:, o:o + CONV_W]
    u = p[:, o + CONV_W:o + 2 * CONV_W] * p[:, o + 2 * CONV_W:o + 3 * CONV_W]
    return q, k, v, gate_b, u


def _inproj_main_kernel(tiles_per_seq, x_ref, g_ref, w_ref, cw_ref, init_ref,
                        q_ref, kt_ref, v4_ref, kb_ref, vb_ref, c_ref, tail_ref, ubuf):
    tile = x_ref.shape[0]

    @pl.when(pl.program_id(0) % tiles_per_seq == 0)
    def _():
        ubuf[0:SUBLANES, :] = init_ref[...]

    q, k, v, gate_b, u = _in_projection(x_ref[...], g_ref[...], w_ref, QK_SCALE * LOG2E)
    q_ref[...] = q.astype(BF16)
    kt_ref[...] = k.T
    for h in range(HEADS):
        v4_ref[pl.ds(h, tile, stride=HEADS), :] = v[:, h * V_DIM:(h + 1) * V_DIM]
    kb_ref[...] = k.astype(BF16)
    vb_ref[...] = v.astype(BF16)
    ubuf[SUBLANES:SUBLANES + tile, :] = u
    u1 = ubuf[SUBLANES - 1:SUBLANES - 1 + tile, :]
    u2 = ubuf[SUBLANES - 2:SUBLANES - 2 + tile, :]
    y = cw_ref[0:1, :] * u2 + cw_ref[1:2, :] * u1 + cw_ref[2:3, :] * u
    c_ref[...] = (gate_b * y).astype(BF16)
    tail_ref[...] = u[tile - 2:, :]
    ubuf[0:SUBLANES, :] = u[tile - SUBLANES:, :]


def _inproj_extra_kernel(x_ref, g_ref, w_ref, cw_ref, p1_ref, p2_ref, m1_ref, m2_ref,
                         q_ref, kf_ref, vf_ref, c_ref, u_ref, ubuf):
    tile = x_ref.shape[0]
    q, k, v, gate_b, u = _in_projection(x_ref[...], g_ref[...], w_ref, QK_SCALE)
    q_ref[...] = q
    kf_ref[...] = k
    vf_ref[...] = v
    u_ref[...] = u
    ubuf[0:SUBLANES, :] = jnp.zeros((SUBLANES, CONV_W), F32)
    ubuf[SUBLANES:SUBLANES + tile, :] = u
    u1 = jnp.where(m1_ref[...] > 0.5, p1_ref[...], ubuf[SUBLANES - 1:SUBLANES - 1 + tile, :])
    u2 = jnp.where(m2_ref[...] > 0.5, p2_ref[...], ubuf[SUBLANES - 2:SUBLANES - 2 + tile, :])
    y = cw_ref[0:1, :] * u2 + cw_ref[1:2, :] * u1 + cw_ref[2:3, :] * u
    c_ref[...] = (gate_b * y).astype(BF16)


def _inproj_main_call(x1, g, w_in, conv_w, init, tile, rows_per_seq):
    rows = x1.shape[0]
    tiles_per_seq = rows_per_seq // tile
    n_seq = rows // rows_per_seq
    sd = jax.ShapeDtypeStruct
    return pl.pallas_call(
        functools.partial(_inproj_main_kernel, tiles_per_seq),
        grid=(rows // tile,),
        in_specs=[_row_spec(tile, D_MODEL), _const_spec((1, D_MODEL)), _const_spec((D_MODEL, IN_COLS)),
                  _const_spec((3, CONV_W)), _const_spec((SUBLANES, CONV_W))],
        out_specs=[_row_spec(tile, ATTN_W),
                   pl.BlockSpec((None, ATTN_W, tile), lambda i: (i // tiles_per_seq, 0, i % tiles_per_seq)),
                   _row_spec(tile * HEADS, V_DIM),
                   _row_spec(tile, ATTN_W), _row_spec(tile, ATTN_W), _row_spec(tile, CONV_W),
                   pl.BlockSpec((None, 2, CONV_W), lambda i: (i // tiles_per_seq, 0, 0))],
        out_shape=[sd((rows, ATTN_W), BF16), sd((n_seq, ATTN_W, rows_per_seq), F32),
                   sd((rows * HEADS, V_DIM), F32),
                   sd((rows, ATTN_W), BF16), sd((rows, ATTN_W), BF16), sd((rows, CONV_W), BF16),
                   sd((n_seq, 2, CONV_W), F32)],
        scratch_shapes=[pltpu.VMEM((SUBLANES + tile, CONV_W), F32)],
        compiler_params=_row_params(),
        name="inproj_main",
    )(x1, g, w_in, conv_w, init)


def _inproj_extra_call(x1, g, w_in, conv_w, p1, p2, m1, m2):
    rows = x1.shape[0]
    sd = jax.ShapeDtypeStruct
    full = lambda cols: pl.BlockSpec((rows, cols), lambda i: (0, 0))
    return pl.pallas_call(
        _inproj_extra_kernel,
        grid=(1,),
        in_specs=[full(D_MODEL), _const_spec((1, D_MODEL)), _const_spec((D_MODEL, IN_COLS)),
                  _const_spec((3, CONV_W))] + [full(CONV_W)] * 4,
        out_specs=[full(ATTN_W)] * 3 + [full(CONV_W)] * 2,
        out_shape=[sd((rows, ATTN_W), F32)] * 3 + [sd((rows, CONV_W), BF16), sd((rows, CONV_W), F32)],
        scratch_shapes=[pltpu.VMEM((SUBLANES + rows, CONV_W), F32)],
        compiler_params=_row_params(),
        name="inproj_extra",
    )(x1, g, w_in, conv_w, p1, p2, m1, m2)


def _stack_maps(q):
    lane = lax.broadcasted_iota(jnp.int32, q.shape, 1)
    zero = jnp.zeros_like(q)
    return jnp.concatenate([jnp.where(lane < HEAD_DIM, q, zero), jnp.where(lane >= HEAD_DIM, q, zero)], axis=0)


def _prompt_attn_kernel(q_ref, k_ref, v_ref, km_ref, vm_ref, lq1_ref, lk1_ref, lq2_ref, lk2_ref, g_ref,
                        o_ref, m_sc, acc_sc):
    tq = q_ref.shape[0]
    i = pl.program_id(1)
    n_chunks = 2 * tq // ROW_CHUNK
    qs = [_stack_maps(q_ref[:, h * V_DIM:(h + 1) * V_DIM]) for h in range(HEADS)]

    def update(h, c, s, v_aug, first):
        rows = slice(c * ROW_CHUNK, (c + 1) * ROW_CHUNK)
        m_cur = jnp.max(s, axis=1, keepdims=True)
        if first:
            m_new = jnp.broadcast_to(m_cur, (ROW_CHUNK, LANES))
            p = jnp.exp2(s - m_cur)
            acc_sc[h, rows, :] = _dot(p.astype(BF16), v_aug)
        else:
            m_prev = m_sc[h, rows, :]
            m_new = jnp.maximum(m_prev, m_cur)
            alpha = jnp.exp2(m_prev - m_new)
            reps = s.shape[1] // LANES
            p = jnp.exp2(s - jnp.concatenate([m_new] * reps, axis=1))
            acc_sc[h, rows, :] = (jnp.concatenate([alpha, alpha], axis=1) * acc_sc[h, rows, :]
                                  + _dot(p.astype(BF16), v_aug))
        m_sc[h, rows, :] = m_new

    def block(k_blk, v_blk, mask_fn, first):
        for h in range(HEADS):
            k = k_blk(h)
            v_aug = jnp.concatenate([v_blk(h), jnp.ones(k.shape, BF16)], axis=1)
            for c in range(n_chunks):
                s = _dot_nt(qs[h][c * ROW_CHUNK:(c + 1) * ROW_CHUNK], k)
                if mask_fn is not None:
                    s = jnp.where(mask_fn(c), s, NEG_INF)
                update(h, c, s, v_aug, first)

    def kv_block(j):
        start = pl.multiple_of(j * KV_TILE, KV_TILE)
        return (lambda h: k_ref[pl.ds(start, KV_TILE), h * V_DIM:(h + 1) * V_DIM],
                lambda h: v_ref[pl.ds(start, KV_TILE), h * V_DIM:(h + 1) * V_DIM])

    k_diag, v_diag = kv_block(i)
    row = lax.broadcasted_iota(jnp.int32, (ROW_CHUNK, KV_TILE + PAGE), 0)
    col = lax.broadcasted_iota(jnp.int32, (ROW_CHUNK, KV_TILE + PAGE), 1)
    block(lambda h: jnp.concatenate([k_diag(h), km_ref[:, h * V_DIM:(h + 1) * V_DIM]], axis=0),
          lambda h: jnp.concatenate([v_diag(h), vm_ref[:, h * V_DIM:(h + 1) * V_DIM]], axis=0),
          lambda c: (col <= (row + c * ROW_CHUNK) % tq) | ((col >= KV_TILE) & (col < KV_TILE + N_META)),
          True)

    rem = i % KV_UNROLL
    width = 1
    while width < KV_UNROLL:
        def leftover(width=width):
            first_blk = (i // KV_UNROLL) * KV_UNROLL + (rem & (width - 1))
            for d in range(width):
                block(*kv_block(first_blk + d), None, False)
        pl.when(rem & width != 0)(leftover)
        width *= 2

    def full_group(jj, carry):
        for d in range(KV_UNROLL):
            block(*kv_block(KV_UNROLL * jj + d), None, False)
        return carry

    lax.fori_loop(0, i // KV_UNROLL, full_group, 0)

    lam = _lambda(lq1_ref, lk1_ref, lq2_ref, lk2_ref)
    for h in range(HEADS):
        acc = acc_sc[h]
        o = acc[:, :V_DIM] / acc[:, V_DIM:]
        o_ref[:, h * V_DIM:(h + 1) * V_DIM] = _head_out(o[:tq], o[tq:], lam, g_ref[...]).astype(o_ref.dtype)


def _prompt_attn_call(q, k, v, km, vm, lq1, lk1, lq2, lk2, subln_g):
    b, s, _ = q.shape
    small = lambda shape: pl.BlockSpec(shape, lambda b_, i: (0,) * len(shape))
    resident = pl.BlockSpec((None, s, ATTN_W), lambda b_, i: (b_, 0, 0), pipeline_mode=pl.Buffered(1))
    return pl.pallas_call(
        _prompt_attn_kernel,
        grid=(b, s // Q_TILE),
        in_specs=[pl.BlockSpec((None, Q_TILE, ATTN_W), lambda b_, i: (b_, i, 0)), resident, resident,
                  small((PAGE, ATTN_W)), small((PAGE, ATTN_W)),
                  small((1, HEAD_DIM)), small((1, HEAD_DIM)), small((1, HEAD_DIM)), small((1, HEAD_DIM)),
                  small((1, V_DIM))],
        out_specs=pl.BlockSpec((None, Q_TILE, ATTN_W), lambda b_, i: (b_, i, 0)),
        out_shape=jax.ShapeDtypeStruct((b, s, ATTN_W), BF16),
        scratch_shapes=[pltpu.VMEM((HEADS, 2 * Q_TILE, LANES), F32),
                        pltpu.VMEM((HEADS, 2 * Q_TILE, 2 * V_DIM), F32)],
        compiler_params=pltpu.CompilerParams(dimension_semantics=("arbitrary",) * 2,
                                             vmem_limit_bytes=VMEM_LIMIT),
        name="prompt_attn",
    )(q, k, v, km, vm, lq1, lk1, lq2, lk2, subln_g)


def _meta_attn_kernel(q_ref, k_ref, v_ref, lq1_ref, lk1_ref, lq2_ref, lk2_ref, g_ref, o_ref):
    t = q_ref.shape[0]
    qs = _stack_maps(q_ref[...].astype(BF16))
    pad = jnp.zeros((PAGE - t, V_DIM), F32)
    k = jnp.concatenate([k_ref[...], pad], axis=0).astype(BF16)
    v = jnp.concatenate([v_ref[...], pad], axis=0).astype(BF16)
    s = _dot_nt(qs, k)
    row = lax.broadcasted_iota(jnp.int32, s.shape, 0)
    col = lax.broadcasted_iota(jnp.int32, s.shape, 1)
    qrow = jnp.where(row >= t, row - t, row)
    s = jnp.where(col <= qrow, s, NEG_INF)
    p = jnp.exp(s - jnp.max(s, axis=-1, keepdims=True))
    o = _dot(p.astype(BF16), v) / jnp.sum(p, axis=-1, keepdims=True)
    lam = _lambda(lq1_ref, lk1_ref, lq2_ref, lk2_ref)
    o_ref[...] = _head_out(o[:t], o[t:], lam, g_ref[...])


def _meta_attn_call(q, k, v, lq1, lk1, lq2, lk2, subln_g):
    small = lambda shape: pl.BlockSpec(shape, lambda h: (0,) * len(shape))
    head = pl.BlockSpec((N_META, V_DIM), lambda h: (0, h))
    return pl.pallas_call(
        _meta_attn_kernel,
        grid=(HEADS,),
        in_specs=[head, head, head, small((1, HEAD_DIM)), small((1, HEAD_DIM)), small((1, HEAD_DIM)),
                  small((1, HEAD_DIM)), small((1, V_DIM))],
        out_specs=head,
        out_shape=jax.ShapeDtypeStruct((N_META, ATTN_W), F32),
        compiler_params=pltpu.CompilerParams(dimension_semantics=("arbitrary",)),
        name="meta_attn",
    )(q, k, v, lq1, lk1, lq2, lk2, subln_g)


def kernel(x_prompt, x_sample, cache_k, cache_v, state_conv, page_table, meta_tokens, ffn1_pre_g, ffn1_w_gate, ffn1_w_up, ffn1_w_down, ffn1_post_g, mix_pre_g, w_in, lambda_q1, lambda_k1, lambda_q2, lambda_k2, subln_g, conv_w, w_out, mix_post_g, ffn2_pre_g, ffn2_w_gate, ffn2_w_up, ffn2_w_down, ffn2_post_g):
    batch, seq, _ = x_prompt.shape
    n_seq, t_new, _ = x_sample.shape
    n_pool = cache_k.shape[1]
    l = 0

    bf = lambda w: w[l].astype(BF16)
    f1 = (ffn1_pre_g, bf(ffn1_w_gate), bf(ffn1_w_up), bf(ffn1_w_down), ffn1_post_g)
    f2 = (ffn2_pre_g, bf(ffn2_w_gate), bf(ffn2_w_up), bf(ffn2_w_down), ffn2_post_g)
    w_in_b, w_out_b = bf(w_in), bf(w_out)
    lam_args = (lambda_q1, lambda_k1, lambda_q2, lambda_k2, subln_g)
    ckt = jnp.transpose(cache_k[l], (0, 2, 3, 4, 1)).reshape(n_pool, ATTN_W, PAGE)
    cv = cache_v[l].reshape(n_pool, PAGE * HEADS, V_DIM)

    n_s = n_seq * t_new
    n_x = N_META + n_s
    xe = jnp.concatenate([meta_tokens, x_sample.reshape(n_s, D_MODEL)], axis=0)
    st = state_conv[l]
    zpad = lambda a, n: jnp.concatenate([a, jnp.zeros((n_seq, n, CONV_W), F32)], axis=1).reshape(n_s, CONV_W)
    zmeta = jnp.zeros((N_META, CONV_W), F32)
    p1 = jnp.concatenate([zmeta, zpad(st[:, 1:2], t_new - 1)], axis=0)
    p2 = jnp.concatenate([zmeta, zpad(st, t_new - 2)], axis=0)
    r = jnp.arange(n_x)[:, None]
    in_sample = r >= N_META
    pos = jnp.where(in_sample, (r - N_META) % t_new, r)
    m1 = jnp.broadcast_to((pos < 1).astype(F32), (n_x, CONV_W))
    m2 = jnp.broadcast_to((pos < 2).astype(F32), (n_x, CONV_W))

    xe1 = _ffn_stage_call(xe, f1, n_x, name="ffn1_extra")
    qe, ke, ve, ce, ue = _inproj_extra_call(xe1, mix_pre_g, w_in_b, conv_w[l], p1, p2, m1, m2)
    ae_meta = _meta_attn_call(qe[:N_META], ke[:N_META], ve[:N_META], *lam_args)

    rows = batch * seq
    half = n_seq // 2
    host = lambda seq0: (page_table, seq0, half, qe, ke, ve, N_META, t_new, ckt, cv, lam_args)
    xp = x_prompt.reshape(rows, D_MODEL)
    xp1, ae_s0 = _ffn_stage_call(xp, f1, HOST_ROW_TILE, host=host(0), name="ffn1")
    init = jnp.concatenate([jnp.zeros((SUBLANES - 2, CONV_W), F32), ue[N_META - 2:N_META]], axis=0)
    qp, kpt, vp4, kpb, vpb, cp, tail = _inproj_main_call(xp1, mix_pre_g, w_in_b, conv_w[l], init,
                                                        tile=ROW_TILE, rows_per_seq=seq)
    meta_pad = lambda a: jnp.concatenate([a[:N_META], jnp.zeros((PAGE - N_META, ATTN_W), F32)], axis=0).astype(BF16)
    shp = (batch, seq, ATTN_W)
    ap = _prompt_attn_call(qp.reshape(shp), kpb.reshape(shp), vpb.reshape(shp), meta_pad(ke), meta_pad(ve),
                           *lam_args)
    yp, ae_s1 = _ffn_stage_call(xp1, f2, HOST_ROW_TILE, mix=(ap.reshape(rows, ATTN_W), cp, w_out_b, mix_post_g),
                                host=host(half), name="mixout_ffn2")
    ae = jnp.concatenate([ae_meta, ae_s0, ae_s1], axis=0)
    ye = _ffn_stage_call(xe1, f2, n_x, mix=(ae, ce, w_out_b, mix_post_g), name="mixout_ffn2_extra")

    y_prompt = yp.reshape(batch, seq, D_MODEL)
    y_sample = ye[N_META:].reshape(n_seq, t_new, D_MODEL)
    kt_meta = jnp.broadcast_to(ke[:N_META].T[None], (batch, ATTN_W, N_META))
    kt_all = jnp.concatenate([kt_meta, kpt], axis=2).reshape(batch, HEADS, 2, HEAD_DIM, N_META + seq)
    k_prompt_new = jnp.transpose(kt_all, (0, 4, 1, 2, 3))[None]
    v_meta = jnp.broadcast_to(ve[:N_META].reshape(1, N_META, HEADS, V_DIM), (batch, N_META, HEADS, V_DIM))
    v_prompt_new = jnp.concatenate([v_meta, vp4.reshape(batch, seq, HEADS, V_DIM)], axis=1)[None]
    conv_prompt_new = tail[None]
    k_sample_new = ke[N_META:].reshape(1, n_seq, t_new, HEADS, 2, HEAD_DIM)
    v_sample_new = ve[N_META:].reshape(1, n_seq, t_new, HEADS, V_DIM)
    conv_sample_new = ue[N_META:].reshape(n_seq, t_new, CONV_W)[None, :, t_new - 2:, :]
    return (y_prompt, y_sample, k_prompt_new, v_prompt_new, conv_prompt_new,
            k_sample_new, v_sample_new, conv_sample_new)
```

```python
import functools
import math

import jax
import jax.numpy as jnp
from jax import lax
from jax.experimental import pallas as pl
from jax.experimental.pallas import tpu as pltpu

F32 = jnp.float32
BF16 = jnp.bfloat16

D_MODEL = 1024
D_FF = 2816
N_META = 16
HEADS = 4
HEAD_DIM = 64
V_DIM = 2 * HEAD_DIM
ATTN_W = HEADS * V_DIM
CONV_W = D_MODEL - ATTN_W
IN_COLS = 3 * ATTN_W + 3 * CONV_W
PAGE = 128
RMS_EPS = 1e-6
SUBLN_EPS = 1e-5
NEG_INF = -1e30
LAMBDA_INIT = 0.8 - 0.6 * math.exp(-0.3 * 0)
QK_SCALE = HEAD_DIM ** -0.5
LOG2E = math.log2(math.e)

SUBLANES = 8
LANES = 128
VMEM_LIMIT = 56 * 1024 * 1024

ROW_TILE = 512
HOST_ROW_TILE = 512
MXU_TILE = 256
FF_CHUNKS = ((0, 6 * MXU_TILE), (6 * MXU_TILE, D_FF))
KV_TILE = 256
Q_TILE = 2 * KV_TILE
KV_UNROLL = 4


def _rms(x, g, eps):
    ms = jnp.mean(x * x, axis=-1, keepdims=True)
    return x * lax.rsqrt(ms + eps) * g


def _dot(a, b):
    return jnp.dot(a, b, preferred_element_type=F32)


def _dot_nt(a, b):
    return lax.dot_general(a, b, (((1,), (1,)), ((), ())), preferred_element_type=F32)


def _ffn_half_step(x, pre_g, wg_ref, wu_ref, wd_ref, post_g, side_work=()):
    side_work = list(side_work)
    run_side = lambda: side_work.pop(0)() if side_work else None
    h = _rms(x, pre_g, RMS_EPS).astype(BF16)
    acc = None
    for lo, hi in FF_CHUNKS:
        sl = slice(lo, hi)
        g = _dot(h, wg_ref[:, sl])
        run_side()
        u = _dot(h, wu_ref[:, sl])
        run_side()
        a = (g * jax.nn.sigmoid(g) * u).astype(BF16)
        d = _dot(a, wd_ref[sl, :])
        run_side()
        acc = d if acc is None else acc + d
    assert not side_work
    return x + 0.5 * _rms(acc, post_g, RMS_EPS)


def _lambda(lq1_ref, lk1_ref, lq2_ref, lk2_ref):
    a = jnp.exp(jnp.sum(lq1_ref[...] * lk1_ref[...], axis=-1, keepdims=True))
    b = jnp.exp(jnp.sum(lq2_ref[...] * lk2_ref[...], axis=-1, keepdims=True))
    return a - b + LAMBDA_INIT


def _head_out(o1, o2, lam, g):
    o = o1 - lam * o2
    return _rms(o, g, SUBLN_EPS) * (1.0 - LAMBDA_INIT)


class _SampleAttn:
    N_REFS = 8

    def __init__(self, in_refs, k_refs, v_refs, o_ref, qbd_sc, m_sc, l_sc, acc_sc):
        self.q_ref, self.kn_ref, self.vn_ref = in_refs[:3]
        self.lam_refs, self.g_ref = in_refs[3:7], in_refs[7]
        self.k_refs, self.v_refs, self.o_ref = k_refs, v_refs, o_ref
        self.qbd_sc, self.m_sc, self.l_sc, self.acc_sc = qbd_sc, m_sc, l_sc, acc_sc
        self.t = self.q_ref.shape[0]

    def _softmax(self, s):
        m_prev = self.m_sc[...]
        m_new = jnp.maximum(m_prev, jnp.max(s, axis=1, keepdims=True))
        alpha = jnp.exp(m_prev - m_new)
        p = jnp.exp(s - m_new)
        self.l_sc[...] = alpha * self.l_sc[...] + jnp.sum(p, axis=1, keepdims=True)
        self.m_sc[...] = m_new
        return alpha, p.astype(BF16)

    def _values(self, g, alpha, pb, v_pair):
        t, acc_sc = self.t, self.acc_sc
        pv = _dot(pb[g * 4 * t:(g + 1) * 4 * t], v_pair)
        for e in range(2):
            rs = slice((2 * g + e) * 2 * t, (2 * g + e + 1) * 2 * t)
            acc_sc[rs, :] = (alpha[rs] * acc_sc[rs, :]
                             + pv[e * 2 * t:(e + 1) * 2 * t, e * V_DIM:(e + 1) * V_DIM])

    def start(self, j):
        t = self.t
        n_rows = HEADS * 2 * t

        @pl.when(j == 0)
        def _():
            qt = jnp.concatenate([self.q_ref[...]] * (n_rows // t), axis=0)
            row = lax.broadcasted_iota(jnp.int32, qt.shape, 0)
            col = lax.broadcasted_iota(jnp.int32, qt.shape, 1)
            self.qbd_sc[...] = jnp.where(col // HEAD_DIM == row // t, qt, 0.0).astype(BF16)
            self.m_sc[...] = jnp.full(self.m_sc.shape, NEG_INF, F32)
            self.l_sc[...] = jnp.zeros(self.l_sc.shape, F32)
            self.acc_sc[...] = jnp.zeros(self.acc_sc.shape, F32)

    def page_work(self):
        st = {}

        def scores():
            kt = jnp.concatenate([r[...].astype(BF16) for r in self.k_refs], axis=1)
            st["s"] = _dot(self.qbd_sc[...], kt)

        def softmax():
            st["alpha"], st["pb"] = self._softmax(st["s"])

        def values(g):
            head = lambda r, h: r[pl.ds(h, PAGE, stride=HEADS), :].astype(BF16)
            v_pair = jnp.concatenate([jnp.concatenate([head(r, 2 * g), head(r, 2 * g + 1)], axis=1)
                                      for r in self.v_refs], axis=0)
            self._values(g, st["alpha"], st["pb"], v_pair)

        return [scores, softmax] + [functools.partial(values, g) for g in range(HEADS // 2)]

    def finish(self, j, n_steps):
        t = self.t

        @pl.when(j == n_steps - 1)
        def _():
            pad = jnp.zeros((PAGE - t, ATTN_W), F32)
            kn = jnp.concatenate([self.kn_ref[...], pad], axis=0).astype(BF16)
            vn = jnp.concatenate([self.vn_ref[...], pad], axis=0).astype(BF16)
            s = _dot_nt(self.qbd_sc[...], kn)
            row = lax.broadcasted_iota(jnp.int32, s.shape, 0)
            key = lax.broadcasted_iota(jnp.int32, s.shape, 1)
            s = jnp.where((key < t) & (key <= row % t), s, NEG_INF)
            alpha, pb = self._softmax(s)
            for g in range(HEADS // 2):
                self._values(g, alpha, pb, vn[:, g * 2 * V_DIM:(g + 1) * 2 * V_DIM])
            o = self.acc_sc[...] / self.l_sc[...]
            lam = _lambda(*self.lam_refs)
            for h in range(HEADS):
                r0 = h * 2 * t
                self.o_ref[:, h * V_DIM:(h + 1) * V_DIM] = _head_out(
                    o[r0:r0 + t], o[r0 + t:r0 + 2 * t], lam, self.g_ref[...])


def _ffn_stage_kernel(has_mix, host, *refs):
    refs = list(refs)
    take = lambda n: [refs.pop(0) for _ in range(n)]
    if host:
        (pt_ref,) = take(1)
    (x_ref,) = take(1)
    if has_mix:
        a_ref, c_ref, wo_ref, mpost_ref = take(4)
    pre_ref, wg_ref, wu_ref, wd_ref, post_ref = take(5)
    if host:
        sample_refs = take(_SampleAttn.N_REFS)
        kt_hbm, v_hbm = take(2)
    (o_ref,) = take(1)
    side_work = ()
    if host:
        pps, host_steps, seq0, n_pages = host
        ao_ref, kbuf, vbuf, sems = take(4)
        i = pl.program_id(0)
        j = i % host_steps
        first_page = (seq0 + i // host_steps) * n_pages + j * pps
        copies = []
        for r in range(pps):
            page = pt_ref[first_page + r]
            copies.append(pltpu.make_async_copy(kt_hbm.at[page], kbuf.at[r], sems.at[r]))
            copies.append(pltpu.make_async_copy(v_hbm.at[page], vbuf.at[r], sems.at[pps + r]))
        for c in copies:
            c.start()
        sample = _SampleAttn(sample_refs, [kbuf.at[r] for r in range(pps)], [vbuf.at[r] for r in range(pps)],
                             ao_ref, *refs)
        sample.start(j)

        def arrive():
            for c in copies:
                c.wait()

        pieces = sample.page_work()
        side_work = [lambda: None, lambda: None, lambda: (arrive(), pieces[0]())] + pieces[1:2] + [
            lambda: [p() for p in pieces[2:]]]
    x = x_ref[...]
    if has_mix:
        m = _dot(a_ref[...].astype(BF16), wo_ref[:ATTN_W, :]) + _dot(c_ref[...], wo_ref[ATTN_W:, :])
        x = x + _rms(m, mpost_ref[...], RMS_EPS)
    o_ref[...] = _ffn_half_step(x, pre_ref[...], wg_ref, wu_ref, wd_ref, post_ref[...], side_work)
    if host:
        sample.finish(j, host_steps)


def _const_spec(shape):
    return pl.BlockSpec(shape, lambda *_: (0,) * len(shape), pipeline_mode=pl.Buffered(1))


def _row_spec(tile, cols):
    return pl.BlockSpec((tile, cols), lambda i, *_: (i, 0))


def _ffn_weight_specs():
    return [_const_spec((1, D_MODEL)), _const_spec((D_MODEL, D_FF)), _const_spec((D_MODEL, D_FF)),
            _const_spec((D_FF, D_MODEL)), _const_spec((1, D_MODEL))]


def _row_params():
    return pltpu.CompilerParams(dimension_semantics=("arbitrary",), vmem_limit_bytes=VMEM_LIMIT)


def _ffn_stage_call(x, ffn, tile, mix=None, host=None, name="ffn"):
    rows = x.shape[0]
    steps = rows // tile
    args = [x]
    in_specs = [_row_spec(tile, D_MODEL)]
    if mix is not None:
        args += list(mix)
        in_specs += [_row_spec(tile, ATTN_W), _row_spec(tile, CONV_W),
                     _const_spec((D_MODEL, D_MODEL)), _const_spec((1, D_MODEL))]
    args += list(ffn)
    in_specs += _ffn_weight_specs()
    out_specs = _row_spec(tile, D_MODEL)
    out_shape = jax.ShapeDtypeStruct((rows, D_MODEL), F32)
    if host is None:
        return pl.pallas_call(
            functools.partial(_ffn_stage_kernel, mix is not None, None),
            grid=(steps,), in_specs=in_specs, out_specs=out_specs, out_shape=out_shape,
            compiler_params=_row_params(), name=name,
        )(*args)

    page_table, seq0, n_host_seq, q, k_new, v_new, row0, t, cache_kt, cache_v, lam_args = host
    n_pages = page_table.shape[1]
    assert steps % n_host_seq == 0 and row0 % t == 0
    host_steps = steps // n_host_seq
    pps = n_pages // host_steps
    assert pps * host_steps == n_pages
    blk0 = row0 // t + seq0
    n_rows = HEADS * 2 * t
    seq_spec = pl.BlockSpec((t, ATTN_W), lambda i, pt: (blk0 + i // host_steps, 0))
    small = lambda shape: pl.BlockSpec(shape, lambda i, pt: (0,) * len(shape))
    in_hbm = pl.BlockSpec(memory_space=pl.ANY)

    args += [q, k_new, v_new, *lam_args, cache_kt, cache_v]
    in_specs += [seq_spec] * 3 + [small((1, HEAD_DIM))] * 4 + [small((1, V_DIM))] + [in_hbm] * 2
    grid_spec = pltpu.PrefetchScalarGridSpec(
        num_scalar_prefetch=1,
        grid=(steps,),
        in_specs=in_specs,
        out_specs=[out_specs, pl.BlockSpec((t, ATTN_W), lambda i, pt: (i // host_steps, 0))],
        scratch_shapes=[pltpu.VMEM((pps, ATTN_W, PAGE), F32), pltpu.VMEM((pps, PAGE * HEADS, V_DIM), F32),
                        pltpu.SemaphoreType.DMA((2 * pps,)),
                        pltpu.VMEM((n_rows, ATTN_W), BF16), pltpu.VMEM((n_rows, 1), F32),
                        pltpu.VMEM((n_rows, 1), F32), pltpu.VMEM((n_rows, V_DIM), F32)],
    )
    return pl.pallas_call(
        functools.partial(_ffn_stage_kernel, mix is not None, (pps, host_steps, seq0, n_pages)),
        grid_spec=grid_spec,
        out_shape=[out_shape, jax.ShapeDtypeStruct((n_host_seq * t, ATTN_W), F32)],
        compiler_params=_row_params(), name=name,
    )(page_table.reshape(-1), *args)


def _in_projection(x, g, w_ref, q_scale):
    h = _rms(x, g, RMS_EPS).astype(BF16)
    p = _dot(h, w_ref[...])
    q = p[:, :ATTN_W] * q_scale
    k = p[:, ATTN_W:2 * ATTN_W]
    v = p[:, 2 * ATTN_W:3 * ATTN_W]
    o = 3 * ATTN_W
    gate_b = p[:, o:o + CONV_W]
    u = p[:, o + CONV_W:o + 2 * CONV_W] * p[:, o + 2 * CONV_W:o + 3 * CONV_W]
    return q, k, v, gate_b, u


def _inproj_main_kernel(tiles_per_seq, x_ref, g_ref, w_ref, cw_ref, init_ref,
                        q_ref, kt_ref, v4_ref, kb_ref, vb_ref, c_ref, tail_ref, ubuf):
    tile = x_ref.shape[0]

    @pl.when(pl.program_id(0) % tiles_per_seq == 0)
    def _():
        ubuf[0:SUBLANES, :] = init_ref[...]

    q, k, v, gate_b, u = _in_projection(x_ref[...], g_ref[...], w_ref, QK_SCALE * LOG2E)
    q_ref[...] = q.astype(BF16)
    kt_ref[...] = k.T
    for h in range(HEADS):
        v4_ref[pl.ds(h, tile, stride=HEADS), :] = v[:, h * V_DIM:(h + 1) * V_DIM]
    kb_ref[...] = k.astype(BF16)
    vb_ref[...] = v.astype(BF16)
    ubuf[SUBLANES:SUBLANES + tile, :] = u
    u1 = ubuf[SUBLANES - 1:SUBLANES - 1 + tile, :]
    u2 = ubuf[SUBLANES - 2:SUBLANES - 2 + tile, :]
    y = cw_ref[0:1, :] * u2 + cw_ref[1:2, :] * u1 + cw_ref[2:3, :] * u
    c_ref[...] = (gate_b * y).astype(BF16)
    tail_ref[...] = u[tile - 2:, :]
    ubuf[0:SUBLANES, :] = u[tile - SUBLANES:, :]


def _inproj_extra_kernel(x_ref, g_ref, w_ref, cw_ref, p1_ref, p2_ref, m1_ref, m2_ref,
                         q_ref, kf_ref, vf_ref, c_ref, u_ref, ubuf):
    tile = x_ref.shape[0]
    q, k, v, gate_b, u = _in_projection(x_ref[...], g_ref[...], w_ref, QK_SCALE)
    q_ref[...] = q
    kf_ref[...] = k
    vf_ref[...] = v
    u_ref[...] = u
    ubuf[0:SUBLANES, :] = jnp.zeros((SUBLANES, CONV_W), F32)
    ubuf[SUBLANES:SUBLANES + tile, :] = u
    u1 = jnp.where(m1_ref[...] > 0.5, p1_ref[...], ubuf[SUBLANES - 1:SUBLANES - 1 + tile, :])
    u2 = jnp.where(m2_ref[...] > 0.5, p2_ref[...], ubuf[SUBLANES - 2:SUBLANES - 2 + tile, :])
    y = cw_ref[0:1, :] * u2 + cw_ref[1:2, :] * u1 + cw_ref[2:3, :] * u
    c_ref[...] = (gate_b * y).astype(BF16)


def _inproj_main_call(x1, g, w_in, conv_w, init, tile, rows_per_seq):
    rows = x1.shape[0]
    tiles_per_seq = rows_per_seq // tile
    n_seq = rows // rows_per_seq
    sd = jax.ShapeDtypeStruct
    return pl.pallas_call(
        functools.partial(_inproj_main_kernel, tiles_per_seq),
        grid=(rows // tile,),
        in_specs=[_row_spec(tile, D_MODEL), _const_spec((1, D_MODEL)), _const_spec((D_MODEL, IN_COLS)),
                  _const_spec((3, CONV_W)), _const_spec((SUBLANES, CONV_W))],
        out_specs=[_row_spec(tile, ATTN_W),
                   pl.BlockSpec((None, ATTN_W, tile), lambda i: (i // tiles_per_seq, 0, i % tiles_per_seq)),
                   _row_spec(tile * HEADS, V_DIM),
                   _row_spec(tile, ATTN_W), _row_spec(tile, ATTN_W), _row_spec(tile, CONV_W),
                   pl.BlockSpec((None, 2, CONV_W), lambda i: (i // tiles_per_seq, 0, 0))],
        out_shape=[sd((rows, ATTN_W), BF16), sd((n_seq, ATTN_W, rows_per_seq), F32),
                   sd((rows * HEADS, V_DIM), F32),
                   sd((rows, ATTN_W), BF16), sd((rows, ATTN_W), BF16), sd((rows, CONV_W), BF16),
                   sd((n_seq, 2, CONV_W), F32)],
        scratch_shapes=[pltpu.VMEM((SUBLANES + tile, CONV_W), F32)],
        compiler_params=_row_params(),
        name="inproj_main",
    )(x1, g, w_in, conv_w, init)


def _inproj_extra_call(x1, g, w_in, conv_w, p1, p2, m1, m2):
    rows = x1.shape[0]
    sd = jax.ShapeDtypeStruct
    full = lambda cols: pl.BlockSpec((rows, cols), lambda i: (0, 0))
    return pl.pallas_call(
        _inproj_extra_kernel,
        grid=(1,),
        in_specs=[full(D_MODEL), _const_spec((1, D_MODEL)), _const_spec((D_MODEL, IN_COLS)),
                  _const_spec((3, CONV_W))] + [full(CONV_W)] * 4,
        out_specs=[full(ATTN_W)] * 3 + [full(CONV_W)] * 2,
        out_shape=[sd((rows, ATTN_W), F32)] * 3 + [sd((rows, CONV_W), BF16), sd((rows, CONV_W), F32)],
        scratch_shapes=[pltpu.VMEM((SUBLANES + rows, CONV_W), F32)],
        compiler_params=_row_params(),
        name="inproj_extra",
    )(x1, g, w_in, conv_w, p1, p2, m1, m2)


def _stack_maps(q):
    lane = lax.broadcasted_iota(jnp.int32, q.shape, 1)
    zero = jnp.zeros_like(q)
    return jnp.concatenate([jnp.where(lane < HEAD_DIM, q, zero), jnp.where(lane >= HEAD_DIM, q, zero)], axis=0)


def _prompt_attn_kernel(q_ref, k_ref, v_ref, km_ref, vm_ref, lq1_ref, lk1_ref, lq2_ref, lk2_ref, g_ref,
                        o_ref, m_sc, acc_sc):
    tq = q_ref.shape[0]
    i = pl.program_id(1)
    qs = [_stack_maps(q_ref[:, h * V_DIM:(h + 1) * V_DIM]) for h in range(HEADS)]
    all_rows = [(slice(0, 2 * tq), None)]
    half_rows = [tuple(slice(m * tq + e * KV_TILE, m * tq + (e + 1) * KV_TILE) for m in range(2)) for e in range(2)]

    def update(h, rows, s, v_aug, first):
        m_cur = jnp.max(s, axis=1, keepdims=True)
        if first:
            m_new = jnp.broadcast_to(m_cur, (s.shape[0], LANES))
            p = jnp.exp2(s - m_cur)
            acc_sc[h, rows, :] = _dot(p.astype(BF16), v_aug)
        else:
            m_prev = m_sc[h, rows, :]
            m_new = jnp.maximum(m_prev, m_cur)
            alpha = jnp.exp2(m_prev - m_new)
            reps = s.shape[1] // LANES
            p = jnp.exp2(s - jnp.concatenate([m_new] * reps, axis=1))
            acc_sc[h, rows, :] = (jnp.concatenate([alpha, alpha], axis=1) * acc_sc[h, rows, :]
                                  + _dot(p.astype(BF16), v_aug))
        m_sc[h, rows, :] = m_new

    def block(k_blk, v_blk, row_groups, first):
        for h in range(HEADS):
            k = k_blk(h)
            v_aug = jnp.concatenate([v_blk(h), jnp.ones(k.shape, BF16)], axis=1)
            for rows, mask in row_groups:
                s = _dot_nt(qs[h][rows], k)
                if mask is not None:
                    s = jnp.where(mask, s, NEG_INF)
                update(h, rows, s, v_aug, first)

    def kv_block(j):
        start = pl.multiple_of(j * KV_TILE, KV_TILE)
        return (lambda h: k_ref[pl.ds(start, KV_TILE), h * V_DIM:(h + 1) * V_DIM],
                lambda h: v_ref[pl.ds(start, KV_TILE), h * V_DIM:(h + 1) * V_DIM])

    tiles_per_q = tq // KV_TILE
    n_full = i * tiles_per_q
    k_a, v_a = kv_block(n_full)
    row = lax.broadcasted_iota(jnp.int32, (KV_TILE, KV_TILE + PAGE), 0)
    col = lax.broadcasted_iota(jnp.int32, (KV_TILE, KV_TILE + PAGE), 1)
    meta_ok = (col >= KV_TILE) & (col < KV_TILE + N_META)
    first_masks = [((col <= row + e * KV_TILE) & (col < KV_TILE)) | meta_ok for e in range(2)]
    block(lambda h: jnp.concatenate([k_a(h), km_ref[:, h * V_DIM:(h + 1) * V_DIM]], axis=0),
          lambda h: jnp.concatenate([v_a(h), vm_ref[:, h * V_DIM:(h + 1) * V_DIM]], axis=0),
          [(rows, first_masks[e]) for e in range(2) for rows in half_rows[e]], True)
    causal = (lax.broadcasted_iota(jnp.int32, (KV_TILE, KV_TILE), 1)
              <= lax.broadcasted_iota(jnp.int32, (KV_TILE, KV_TILE), 0))
    block(*kv_block(n_full + 1), [(rows, causal) for rows in half_rows[1]], False)

    rem = n_full % KV_UNROLL
    width = tiles_per_q
    while width < KV_UNROLL:
        def leftover(width=width):
            first_blk = (n_full // KV_UNROLL) * KV_UNROLL + (rem & (width - 1))
            for d in range(width):
                block(*kv_block(first_blk + d), all_rows, False)
        pl.when(rem & width != 0)(leftover)
        width *= 2

    def full_group(jj, carry):
        for d in range(KV_UNROLL):
            block(*kv_block(KV_UNROLL * jj + d), all_rows, False)
        return carry

    lax.fori_loop(0, n_full // KV_UNROLL, full_group, 0)

    lam = _lambda(lq1_ref, lk1_ref, lq2_ref, lk2_ref)
    for h in range(HEADS):
        acc = acc_sc[h]
        o = acc[:, :V_DIM] / acc[:, V_DIM:]
        o_ref[:, h * V_DIM:(h + 1) * V_DIM] = _head_out(o[:tq], o[tq:], lam, g_ref[...]).astype(o_ref.dtype)


def _prompt_attn_call(q, k, v, km, vm, lq1, lk1, lq2, lk2, subln_g):
    b, s, _ = q.shape
    small = lambda shape: pl.BlockSpec(shape, lambda b_, i: (0,) * len(shape))
    resident = pl.BlockSpec((None, s, ATTN_W), lambda b_, i: (b_, 0, 0), pipeline_mode=pl.Buffered(1))
    return pl.pallas_call(
        _prompt_attn_kernel,
        grid=(b, s // Q_TILE),
        in_specs=[pl.BlockSpec((None, Q_TILE, ATTN_W), lambda b_, i: (b_, i, 0)), resident, resident,
                  small((PAGE, ATTN_W)), small((PAGE, ATTN_W)),
                  small((1, HEAD_DIM)), small((1, HEAD_DIM)), small((1, HEAD_DIM)), small((1, HEAD_DIM)),
                  small((1, V_DIM))],
        out_specs=pl.BlockSpec((None, Q_TILE, ATTN_W), lambda b_, i: (b_, i, 0)),
        out_shape=jax.ShapeDtypeStruct((b, s, ATTN_W), BF16),
        scratch_shapes=[pltpu.VMEM((HEADS, 2 * Q_TILE, LANES), F32),
                        pltpu.VMEM((HEADS, 2 * Q_TILE, 2 * V_DIM), F32)],
        compiler_params=pltpu.CompilerParams(dimension_semantics=("arbitrary",) * 2,
                                             vmem_limit_bytes=VMEM_LIMIT),
        name="prompt_attn",
    )(q, k, v, km, vm, lq1, lk1, lq2, lk2, subln_g)


def _meta_attn_kernel(q_ref, k_ref, v_ref, lq1_ref, lk1_ref, lq2_ref, lk2_ref, g_ref, o_ref):
    t = q_ref.shape[0]
    qs = _stack_maps(q_ref[...].astype(BF16))
    pad = jnp.zeros((PAGE - t, V_DIM), F32)
    k = jnp.concatenate([k_ref[...], pad], axis=0).astype(BF16)
    v = jnp.concatenate([v_ref[...], pad], axis=0).astype(BF16)
    s = _dot_nt(qs, k)
    row = lax.broadcasted_iota(jnp.int32, s.shape, 0)
    col = lax.broadcasted_iota(jnp.int32, s.shape, 1)
    qrow = jnp.where(row >= t, row - t, row)
    s = jnp.where(col <= qrow, s, NEG_INF)
    p = jnp.exp(s - jnp.max(s, axis=-1, keepdims=True))
    o = _dot(p.astype(BF16), v) / jnp.sum(p, axis=-1, keepdims=True)
    lam = _lambda(lq1_ref, lk1_ref, lq2_ref, lk2_ref)
    o_ref[...] = _head_out(o[:t], o[t:], lam, g_ref[...])


def _meta_attn_call(q, k, v, lq1, lk1, lq2, lk2, subln_g):
    small = lambda shape: pl.BlockSpec(shape, lambda h: (0,) * len(shape))
    head = pl.BlockSpec((N_META, V_DIM), lambda h: (0, h))
    return pl.pallas_call(
        _meta_attn_kernel,
        grid=(HEADS,),
        in_specs=[head, head, head, small((1, HEAD_DIM)), small((1, HEAD_DIM)), small((1, HEAD_DIM)),
                  small((1, HEAD_DIM)), small((1, V_DIM))],
        out_specs=head,
        out_shape=jax.ShapeDtypeStruct((N_META, ATTN_W), F32),
        compiler_params=pltpu.CompilerParams(dimension_semantics=("arbitrary",)),
        name="meta_attn",
    )(q, k, v, lq1, lk1, lq2, lk2, subln_g)


def kernel(x_prompt, x_sample, cache_k, cache_v, state_conv, page_table, meta_tokens, ffn1_pre_g, ffn1_w_gate, ffn1_w_up, ffn1_w_down, ffn1_post_g, mix_pre_g, w_in, lambda_q1, lambda_k1, lambda_q2, lambda_k2, subln_g, conv_w, w_out, mix_post_g, ffn2_pre_g, ffn2_w_gate, ffn2_w_up, ffn2_w_down, ffn2_post_g):
    batch, seq, _ = x_prompt.shape
    n_seq, t_new, _ = x_sample.shape
    n_pool = cache_k.shape[1]
    l = 0

    bf = lambda w: w[l].astype(BF16)
    f1 = (ffn1_pre_g, bf(ffn1_w_gate), bf(ffn1_w_up), bf(ffn1_w_down), ffn1_post_g)
    f2 = (ffn2_pre_g, bf(ffn2_w_gate), bf(ffn2_w_up), bf(ffn2_w_down), ffn2_post_g)
    w_in_b, w_out_b = bf(w_in), bf(w_out)
    lam_args = (lambda_q1, lambda_k1, lambda_q2, lambda_k2, subln_g)
    ckt = jnp.transpose(cache_k[l], (0, 2, 3, 4, 1)).reshape(n_pool, ATTN_W, PAGE)
    cv = cache_v[l].reshape(n_pool, PAGE * HEADS, V_DIM)

    n_s = n_seq * t_new
    n_x = N_META + n_s
    xe = jnp.concatenate([meta_tokens, x_sample.reshape(n_s, D_MODEL)], axis=0)
    st = state_conv[l]
    zpad = lambda a, n: jnp.concatenate([a, jnp.zeros((n_seq, n, CONV_W), F32)], axis=1).reshape(n_s, CONV_W)
    zmeta = jnp.zeros((N_META, CONV_W), F32)
    p1 = jnp.concatenate([zmeta, zpad(st[:, 1:2], t_new - 1)], axis=0)
    p2 = jnp.concatenate([zmeta, zpad(st, t_new - 2)], axis=0)
    r = jnp.arange(n_x)[:, None]
    in_sample = r >= N_META
    pos = jnp.where(in_sample, (r - N_META) % t_new, r)
    m1 = jnp.broadcast_to((pos < 1).astype(F32), (n_x, CONV_W))
    m2 = jnp.broadcast_to((pos < 2).astype(F32), (n_x, CONV_W))

    xe1 = _ffn_stage_call(xe, f1, n_x, name="ffn1_extra")
    qe, ke, ve, ce, ue = _inproj_extra_call(xe1, mix_pre_g, w_in_b, conv_w[l], p1, p2, m1, m2)
    ae_meta = _meta_attn_call(qe[:N_META], ke[:N_META], ve[:N_META], *lam_args)

    rows = batch * seq
    half = n_seq // 2
    host = lambda seq0: (page_table, seq0, half, qe, ke, ve, N_META, t_new, ckt, cv, lam_args)
    xp = x_prompt.reshape(rows, D_MODEL)
    xp1, ae_s0 = _ffn_stage_call(xp, f1, HOST_ROW_TILE, host=host(0), name="ffn1")
    init = jnp.concatenate([jnp.zeros((SUBLANES - 2, CONV_W), F32), ue[N_META - 2:N_META]], axis=0)
    qp, kpt, vp4, kpb, vpb, cp, tail = _inproj_main_call(xp1, mix_pre_g, w_in_b, conv_w[l], init,
                                                        tile=ROW_TILE, rows_per_seq=seq)
    meta_pad = lambda a: jnp.concatenate([a[:N_META], jnp.zeros((PAGE - N_META, ATTN_W), F32)], axis=0).astype(BF16)
    shp = (batch, seq, ATTN_W)
    ap = _prompt_attn_call(qp.reshape(shp), kpb.reshape(shp), vpb.reshape(shp), meta_pad(ke), meta_pad(ve),
                           *lam_args)
    yp, ae_s1 = _ffn_stage_call(xp1, f2, HOST_ROW_TILE, mix=(ap.reshape(rows, ATTN_W), cp, w_out_b, mix_post_g),
                                host=host(half), name="mixout_ffn2")
    ae = jnp.concatenate([ae_meta, ae_s0, ae_s1], axis=0)
    ye = _ffn_stage_call(xe1, f2, n_x, mix=(ae, ce, w_out_b, mix_post_g), name="mixout_ffn2_extra")

    y_prompt = yp.reshape(batch, seq, D_MODEL)
    y_sample = ye[N_META:].reshape(n_seq, t_new, D_MODEL)
    kt_meta = jnp.broadcast_to(ke[:N_META].T[None], (batch, ATTN_W, N_META))
    kt_all = jnp.concatenate([kt_meta, kpt], axis=2).reshape(batch, HEADS, 2, HEAD_DIM, N_META + seq)
    k_prompt_new = jnp.transpose(kt_all, (0, 4, 1, 2, 3))[None]
    v_meta = jnp.broadcast_to(ve[:N_META].reshape(1, N_META, HEADS, V_DIM), (batch, N_META, HEADS, V_DIM))
    v_prompt_new = jnp.concatenate([v_meta, vp4.reshape(batch, seq, HEADS, V_DIM)], axis=1)[None]
    conv_prompt_new = tail[None]
    k_sample_new = ke[N_META:].reshape(1, n_seq, t_new, HEADS, 2, HEAD_DIM)
    v_sample_new = ve[N_META:].reshape(1, n_seq, t_new, HEADS, V_DIM)
    conv_sample_new = ue[N_META:].reshape(n_seq, t_new, CONV_W)[None, :, t_new - 2:, :]
    return (y_prompt, y_sample, k_prompt_new, v_prompt_new, conv_prompt_new,
            k_sample_new, v_sample_new, conv_sample_new)
```

```python
import functools
import math

import jax
import jax.numpy as jnp
from jax import lax
from jax.experimental import pallas as pl
from jax.experimental.pallas import tpu as pltpu

F32 = jnp.float32
BF16 = jnp.bfloat16

D_MODEL = 1024
D_FF = 2816
N_META = 16
HEADS = 4
HEAD_DIM = 64
V_DIM = 2 * HEAD_DIM
ATTN_W = HEADS * V_DIM
CONV_W = D_MODEL - ATTN_W
IN_COLS = 3 * ATTN_W + 3 * CONV_W
PAGE = 128
RMS_EPS = 1e-6
SUBLN_EPS = 1e-5
NEG_INF = -1e30
LAMBDA_INIT = 0.8 - 0.6 * math.exp(-0.3 * 0)
QK_SCALE = HEAD_DIM ** -0.5
LOG2E = math.log2(math.e)

SUBLANES = 8
LANES = 128
VMEM_LIMIT = 56 * 1024 * 1024

ROW_TILE = 512
HOST_ROW_TILE = 512
MXU_TILE = 256
FF_CHUNKS = ((0, 6 * MXU_TILE), (6 * MXU_TILE, D_FF))
KV_TILE = 256
Q_TILE = 2 * KV_TILE
KV_UNROLL = 4


def _rms(x, g, eps):
    ms = jnp.mean(x * x, axis=-1, keepdims=True)
    return x * lax.rsqrt(ms + eps) * g


def _dot(a, b):
    return jnp.dot(a, b, preferred_element_type=F32)


def _dot_nt(a, b):
    return lax.dot_general(a, b, (((1,), (1,)), ((), ())), preferred_element_type=F32)


def _ffn_half_step(x, pre_g, wg_ref, wu_ref, wd_ref, post_g, side_work=()):
    side_work = list(side_work)
    run_side = lambda: side_work.pop(0)() if side_work else None
    h = _rms(x, pre_g, RMS_EPS).astype(BF16)
    acc = None
    for lo, hi in FF_CHUNKS:
        sl = slice(lo, hi)
        g = _dot(h, wg_ref[:, sl])
        run_side()
        u = _dot(h, wu_ref[:, sl])
        run_side()
        a = (g * jax.nn.sigmoid(g) * u).astype(BF16)
        d = _dot(a, wd_ref[sl, :])
        run_side()
        acc = d if acc is None else acc + d
    assert not side_work
    return x + 0.5 * _rms(acc, post_g, RMS_EPS)


def _lambda(lq1_ref, lk1_ref, lq2_ref, lk2_ref):
    a = jnp.exp(jnp.sum(lq1_ref[...] * lk1_ref[...], axis=-1, keepdims=True))
    b = jnp.exp(jnp.sum(lq2_ref[...] * lk2_ref[...], axis=-1, keepdims=True))
    return a - b + LAMBDA_INIT


def _head_out(o1, o2, lam, g):
    o = o1 - lam * o2
    return _rms(o, g, SUBLN_EPS) * (1.0 - LAMBDA_INIT)


class _SampleAttn:
    N_REFS = 8

    def __init__(self, in_refs, k_refs, v_refs, o_ref, qbd_sc, m_sc, l_sc, acc_sc):
        self.q_ref, self.kn_ref, self.vn_ref = in_refs[:3]
        self.lam_refs, self.g_ref = in_refs[3:7], in_refs[7]
        self.k_refs, self.v_refs, self.o_ref = k_refs, v_refs, o_ref
        self.qbd_sc, self.m_sc, self.l_sc, self.acc_sc = qbd_sc, m_sc, l_sc, acc_sc
        self.t = self.q_ref.shape[0]

    def _softmax(self, s):
        m_prev = self.m_sc[...]
        m_new = jnp.maximum(m_prev, jnp.max(s, axis=1, keepdims=True))
        alpha = jnp.exp(m_prev - m_new)
        p = jnp.exp(s - m_new)
        self.l_sc[...] = alpha * self.l_sc[...] + jnp.sum(p, axis=1, keepdims=True)
        self.m_sc[...] = m_new
        return alpha, p.astype(BF16)

    def _values(self, g, alpha, pb, v_pair):
        t, acc_sc = self.t, self.acc_sc
        pv = _dot(pb[g * 4 * t:(g + 1) * 4 * t], v_pair)
        for e in range(2):
            rs = slice((2 * g + e) * 2 * t, (2 * g + e + 1) * 2 * t)
            acc_sc[rs, :] = (alpha[rs] * acc_sc[rs, :]
                             + pv[e * 2 * t:(e + 1) * 2 * t, e * V_DIM:(e + 1) * V_DIM])

    def start(self, j):
        t = self.t
        n_rows = HEADS * 2 * t

        @pl.when(j == 0)
        def _():
            qt = jnp.concatenate([self.q_ref[...]] * (n_rows // t), axis=0)
            row = lax.broadcasted_iota(jnp.int32, qt.shape, 0)
            col = lax.broadcasted_iota(jnp.int32, qt.shape, 1)
            self.qbd_sc[...] = jnp.where(col // HEAD_DIM == row // t, qt, 0.0).astype(BF16)
            self.m_sc[...] = jnp.full(self.m_sc.shape, NEG_INF, F32)
            self.l_sc[...] = jnp.zeros(self.l_sc.shape, F32)
            self.acc_sc[...] = jnp.zeros(self.acc_sc.shape, F32)

    def page_work(self, slots):
        st = {}
        k_refs = [self.k_refs[r] for r in slots]
        v_refs = [self.v_refs[r] for r in slots]

        def scores():
            kt = jnp.concatenate([r[...].astype(BF16) for r in k_refs], axis=1)
            st["s"] = _dot(self.qbd_sc[...], kt)

        def values():
            alpha, pb = self._softmax(st["s"])
            head = lambda r, h: r[pl.ds(h, PAGE, stride=HEADS), :].astype(BF16)
            for g in range(HEADS // 2):
                v_pair = jnp.concatenate([jnp.concatenate([head(r, 2 * g), head(r, 2 * g + 1)], axis=1)
                                          for r in v_refs], axis=0)
                self._values(g, alpha, pb, v_pair)

        return scores, values

    def finish(self, j, n_steps):
        t = self.t

        @pl.when(j == n_steps - 1)
        def _():
            pad = jnp.zeros((PAGE - t, ATTN_W), F32)
            kn = jnp.concatenate([self.kn_ref[...], pad], axis=0).astype(BF16)
            vn = jnp.concatenate([self.vn_ref[...], pad], axis=0).astype(BF16)
            s = _dot_nt(self.qbd_sc[...], kn)
            row = lax.broadcasted_iota(jnp.int32, s.shape, 0)
            key = lax.broadcasted_iota(jnp.int32, s.shape, 1)
            s = jnp.where((key < t) & (key <= row % t), s, NEG_INF)
            alpha, pb = self._softmax(s)
            for g in range(HEADS // 2):
                self._values(g, alpha, pb, vn[:, g * 2 * V_DIM:(g + 1) * 2 * V_DIM])
            o = self.acc_sc[...] / self.l_sc[...]
            lam = _lambda(*self.lam_refs)
            for h in range(HEADS):
                r0 = h * 2 * t
                self.o_ref[:, h * V_DIM:(h + 1) * V_DIM] = _head_out(
                    o[r0:r0 + t], o[r0 + t:r0 + 2 * t], lam, self.g_ref[...])


def _ffn_stage_kernel(has_mix, host, *refs):
    refs = list(refs)
    take = lambda n: [refs.pop(0) for _ in range(n)]
    if host:
        (pt_ref,) = take(1)
    (x_ref,) = take(1)
    if has_mix:
        a_ref, c_ref, wo_ref, mpost_ref = take(4)
    pre_ref, wg_ref, wu_ref, wd_ref, post_ref = take(5)
    if host:
        sample_refs = take(_SampleAttn.N_REFS)
        kt_hbm, v_hbm = take(2)
    (o_ref,) = take(1)
    side_work = ()
    if host:
        pps, host_steps, seq0, n_pages = host
        ao_ref, kbuf, vbuf, sems = take(4)
        i = pl.program_id(0)
        last = pl.num_programs(0) - 1
        j = i % host_steps
        halves = (range(0, pps // 2), range(pps // 2, pps))

        def copies(step, slots):
            first_page = (seq0 + step // host_steps) * n_pages + (step % host_steps) * pps
            out = []
            for r in slots:
                page = pt_ref[first_page + r]
                out.append(pltpu.make_async_copy(kt_hbm.at[page], kbuf.at[r], sems.at[r]))
                out.append(pltpu.make_async_copy(v_hbm.at[page], vbuf.at[r], sems.at[pps + r]))
            return out

        start = lambda cs: [c.start() for c in cs]
        wait = lambda cs: [c.wait() for c in cs]
        first_half, second_half = copies(i, halves[0]), copies(i, halves[1])
        next_first_half = copies(jnp.minimum(i + 1, last), halves[0])

        @pl.when(i == 0)
        def _():
            start(first_half)

        start(second_half)
        sample = _SampleAttn(sample_refs, [kbuf.at[r] for r in range(pps)], [vbuf.at[r] for r in range(pps)],
                             ao_ref, *refs)
        sample.start(j)
        scores_a, values_a = sample.page_work(halves[0])
        scores_b, values_b = sample.page_work(halves[1])
        side_work = [lambda: (wait(first_half), scores_a()), values_a, lambda: start(next_first_half),
                     lambda: (wait(second_half), scores_b()), values_b]
    x = x_ref[...]
    if has_mix:
        m = _dot(a_ref[...].astype(BF16), wo_ref[:ATTN_W, :]) + _dot(c_ref[...], wo_ref[ATTN_W:, :])
        x = x + _rms(m, mpost_ref[...], RMS_EPS)
    o_ref[...] = _ffn_half_step(x, pre_ref[...], wg_ref, wu_ref, wd_ref, post_ref[...], side_work)
    if host:
        sample.finish(j, host_steps)

        @pl.when(i == last)
        def _():
            wait(next_first_half)


def _const_spec(shape):
    return pl.BlockSpec(shape, lambda *_: (0,) * len(shape), pipeline_mode=pl.Buffered(1))


def _row_spec(tile, cols):
    return pl.BlockSpec((tile, cols), lambda i, *_: (i, 0))


def _ffn_weight_specs():
    return [_const_spec((1, D_MODEL)), _const_spec((D_MODEL, D_FF)), _const_spec((D_MODEL, D_FF)),
            _const_spec((D_FF, D_MODEL)), _const_spec((1, D_MODEL))]


def _row_params():
    return pltpu.CompilerParams(dimension_semantics=("arbitrary",), vmem_limit_bytes=VMEM_LIMIT)


def _ffn_stage_call(x, ffn, tile, mix=None, host=None, name="ffn"):
    rows = x.shape[0]
    steps = rows // tile
    args = [x]
    in_specs = [_row_spec(tile, D_MODEL)]
    if mix is not None:
        args += list(mix)
        in_specs += [_row_spec(tile, ATTN_W), _row_spec(tile, CONV_W),
                     _const_spec((D_MODEL, D_MODEL)), _const_spec((1, D_MODEL))]
    args += list(ffn)
    in_specs += _ffn_weight_specs()
    out_specs = _row_spec(tile, D_MODEL)
    out_shape = jax.ShapeDtypeStruct((rows, D_MODEL), F32)
    if host is None:
        return pl.pallas_call(
            functools.partial(_ffn_stage_kernel, mix is not None, None),
            grid=(steps,), in_specs=in_specs, out_specs=out_specs, out_shape=out_shape,
            compiler_params=_row_params(), name=name,
        )(*args)

    page_table, seq0, n_host_seq, q, k_new, v_new, row0, t, cache_kt, cache_v, lam_args = host
    n_pages = page_table.shape[1]
    assert steps % n_host_seq == 0 and row0 % t == 0
    host_steps = steps // n_host_seq
    pps = n_pages // host_steps
    assert pps * host_steps == n_pages
    blk0 = row0 // t + seq0
    n_rows = HEADS * 2 * t
    seq_spec = pl.BlockSpec((t, ATTN_W), lambda i, pt: (blk0 + i // host_steps, 0))
    small = lambda shape: pl.BlockSpec(shape, lambda i, pt: (0,) * len(shape))
    in_hbm = pl.BlockSpec(memory_space=pl.ANY)

    args += [q, k_new, v_new, *lam_args, cache_kt, cache_v]
    in_specs += [seq_spec] * 3 + [small((1, HEAD_DIM))] * 4 + [small((1, V_DIM))] + [in_hbm] * 2
    grid_spec = pltpu.PrefetchScalarGridSpec(
        num_scalar_prefetch=1,
        grid=(steps,),
        in_specs=in_specs,
        out_specs=[out_specs, pl.BlockSpec((t, ATTN_W), lambda i, pt: (i // host_steps, 0))],
        scratch_shapes=[pltpu.VMEM((pps, ATTN_W, PAGE), F32), pltpu.VMEM((pps, PAGE * HEADS, V_DIM), F32),
                        pltpu.SemaphoreType.DMA((2 * pps,)),
                        pltpu.VMEM((n_rows, ATTN_W), BF16), pltpu.VMEM((n_rows, 1), F32),
                        pltpu.VMEM((n_rows, 1), F32), pltpu.VMEM((n_rows, V_DIM), F32)],
    )
    return pl.pallas_call(
        functools.partial(_ffn_stage_kernel, mix is not None, (pps, host_steps, seq0, n_pages)),
        grid_spec=grid_spec,
        out_shape=[out_shape, jax.ShapeDtypeStruct((n_host_seq * t, ATTN_W), F32)],
        compiler_params=_row_params(), name=name,
    )(page_table.reshape(-1), *args)


def _in_projection(x, g, w_ref, q_scale):
    h = _rms(x, g, RMS_EPS).astype(BF16)
    p = _dot(h, w_ref[...])
    q = p[:, :ATTN_W] * q_scale
    k = p[:, ATTN_W:2 * ATTN_W]
    v = p[:, 2 * ATTN_W:3 * ATTN_W]
    o = 3 * ATTN_W
    gate_b = p[:, o:o + CONV_W]
    u = p[:, o + CONV_W:o + 2 * CONV_W] * p[:, o + 2 * CONV_W:o + 3 * CONV_W]
    return q, k, v, gate_b, u


def _inproj_main_kernel(tiles_per_seq, x_ref, g_ref, w_ref, cw_ref, init_ref,
                        q_ref, kt_ref, v4_ref, kb_ref, vb_ref, c_ref, tail_ref, ubuf):
    tile = x_ref.shape[0]

    @pl.when(pl.program_id(0) % tiles_per_seq == 0)
    def _():
        ubuf[0:SUBLANES, :] = init_ref[...]

    q, k, v, gate_b, u = _in_projection(x_ref[...], g_ref[...], w_ref, QK_SCALE * LOG2E)
    q_ref[...] = q.astype(BF16)
    kt_ref[...] = k.T
    for h in range(HEADS):
        v4_ref[pl.ds(h, tile, stride=HEADS), :] = v[:, h * V_DIM:(h + 1) * V_DIM]
    kb_ref[...] = k.astype(BF16)
    vb_ref[...] = v.astype(BF16)
    ubuf[SUBLANES:SUBLANES + tile, :] = u
    u1 = ubuf[SUBLANES - 1:SUBLANES - 1 + tile, :]
    u2 = ubuf[SUBLANES - 2:SUBLANES - 2 + tile, :]
    y = cw_ref[0:1, :] * u2 + cw_ref[1:2, :] * u1 + cw_ref[2:3, :] * u
    c_ref[...] = (gate_b * y).astype(BF16)
    tail_ref[...] = u[tile - 2:, :]
    ubuf[0:SUBLANES, :] = u[tile - SUBLANES:, :]


def _inproj_extra_kernel(x_ref, g_ref, w_ref, cw_ref, p1_ref, p2_ref, m1_ref, m2_ref,
                         q_ref, kf_ref, vf_ref, c_ref, u_ref, ubuf):
    tile = x_ref.shape[0]
    q, k, v, gate_b, u = _in_projection(x_ref[...], g_ref[...], w_ref, QK_SCALE)
    q_ref[...] = q
    kf_ref[...] = k
    vf_ref[...] = v
    u_ref[...] = u
    ubuf[0:SUBLANES, :] = jnp.zeros((SUBLANES, CONV_W), F32)
    ubuf[SUBLANES:SUBLANES + tile, :] = u
    u1 = jnp.where(m1_ref[...] > 0.5, p1_ref[...], ubuf[SUBLANES - 1:SUBLANES - 1 + tile, :])
    u2 = jnp.where(m2_ref[...] > 0.5, p2_ref[...], ubuf[SUBLANES - 2:SUBLANES - 2 + tile, :])
    y = cw_ref[0:1, :] * u2 + cw_ref[1:2, :] * u1 + cw_ref[2:3, :] * u
    c_ref[...] = (gate_b * y).astype(BF16)


def _inproj_main_call(x1, g, w_in, conv_w, init, tile, rows_per_seq):
    rows = x1.shape[0]
    tiles_per_seq = rows_per_seq // tile
    n_seq = rows // rows_per_seq
    sd = jax.ShapeDtypeStruct
    return pl.pallas_call(
        functools.partial(_inproj_main_kernel, tiles_per_seq),
        grid=(rows // tile,),
        in_specs=[_row_spec(tile, D_MODEL), _const_spec((1, D_MODEL)), _const_spec((D_MODEL, IN_COLS)),
                  _const_spec((3, CONV_W)), _const_spec((SUBLANES, CONV_W))],
        out_specs=[_row_spec(tile, ATTN_W),
                   pl.BlockSpec((None, ATTN_W, tile), lambda i: (i // tiles_per_seq, 0, i % tiles_per_seq)),
                   _row_spec(tile * HEADS, V_DIM),
                   _row_spec(tile, ATTN_W), _row_spec(tile, ATTN_W), _row_spec(tile, CONV_W),
                   pl.BlockSpec((None, 2, CONV_W), lambda i: (i // tiles_per_seq, 0, 0))],
        out_shape=[sd((rows, ATTN_W), BF16), sd((n_seq, ATTN_W, rows_per_seq), F32),
                   sd((rows * HEADS, V_DIM), F32),
                   sd((rows, ATTN_W), BF16), sd((rows, ATTN_W), BF16), sd((rows, CONV_W), BF16),
                   sd((n_seq, 2, CONV_W), F32)],
        scratch_shapes=[pltpu.VMEM((SUBLANES + tile, CONV_W), F32)],
        compiler_params=_row_params(),
        name="inproj_main",
    )(x1, g, w_in, conv_w, init)


def _inproj_extra_call(x1, g, w_in, conv_w, p1, p2, m1, m2):
    rows = x1.shape[0]
    sd = jax.ShapeDtypeStruct
    full = lambda cols: pl.BlockSpec((rows, cols), lambda i: (0, 0))
    return pl.pallas_call(
        _inproj_extra_kernel,
        grid=(1,),
        in_specs=[full(D_MODEL), _const_spec((1, D_MODEL)), _const_spec((D_MODEL, IN_COLS)),
                  _const_spec((3, CONV_W))] + [full(CONV_W)] * 4,
        out_specs=[full(ATTN_W)] * 3 + [full(CONV_W)] * 2,
        out_shape=[sd((rows, ATTN_W), F32)] * 3 + [sd((rows, CONV_W), BF16), sd((rows, CONV_W), F32)],
        scratch_shapes=[pltpu.VMEM((SUBLANES + rows, CONV_W), F32)],
        compiler_params=_row_params(),
        name="inproj_extra",
    )(x1, g, w_in, conv_w, p1, p2, m1, m2)


def _stack_maps(q):
    lane = lax.broadcasted_iota(jnp.int32, q.shape, 1)
    zero = jnp.zeros_like(q)
    return jnp.concatenate([jnp.where(lane < HEAD_DIM, q, zero), jnp.where(lane >= HEAD_DIM, q, zero)], axis=0)


def _prompt_attn_kernel(q_ref, k_ref, v_ref, km_ref, vm_ref, lq1_ref, lk1_ref, lq2_ref, lk2_ref, g_ref,
                        o_ref, m_sc, acc_sc):
    tq = q_ref.shape[0]
    i = pl.program_id(1)
    qs = [_stack_maps(q_ref[:, h * V_DIM:(h + 1) * V_DIM]) for h in range(HEADS)]
    all_rows = [(slice(0, 2 * tq), None)]
    half_rows = [tuple(slice(m * tq + e * KV_TILE, m * tq + (e + 1) * KV_TILE) for m in range(2)) for e in range(2)]

    def update(h, rows, s, v_aug, first):
        m_cur = jnp.max(s, axis=1, keepdims=True)
        if first:
            m_new = jnp.broadcast_to(m_cur, (s.shape[0], LANES))
            p = jnp.exp2(s - m_cur)
            acc_sc[h, rows, :] = _dot(p.astype(BF16), v_aug)
        else:
            m_prev = m_sc[h, rows, :]
            m_new = jnp.maximum(m_prev, m_cur)
            alpha = jnp.exp2(m_prev - m_new)
            reps = s.shape[1] // LANES
            p = jnp.exp2(s - jnp.concatenate([m_new] * reps, axis=1))
            acc_sc[h, rows, :] = (jnp.concatenate([alpha, alpha], axis=1) * acc_sc[h, rows, :]
                                  + _dot(p.astype(BF16), v_aug))
        m_sc[h, rows, :] = m_new

    def block(k_blk, v_blk, row_groups, first):
        for h in range(HEADS):
            k = k_blk(h)
            v_aug = jnp.concatenate([v_blk(h), jnp.ones(k.shape, BF16)], axis=1)
            for rows, mask in row_groups:
                s = _dot_nt(qs[h][rows], k)
                if mask is not None:
                    s = jnp.where(mask, s, NEG_INF)
                update(h, rows, s, v_aug, first)

    def kv_block(j):
        start = pl.multiple_of(j * KV_TILE, KV_TILE)
        return (lambda h: k_ref[pl.ds(start, KV_TILE), h * V_DIM:(h + 1) * V_DIM],
                lambda h: v_ref[pl.ds(start, KV_TILE), h * V_DIM:(h + 1) * V_DIM])

    tiles_per_q = tq // KV_TILE
    n_full = i * tiles_per_q
    k_a, v_a = kv_block(n_full)
    row = lax.broadcasted_iota(jnp.int32, (KV_TILE, KV_TILE + PAGE), 0)
    col = lax.broadcasted_iota(jnp.int32, (KV_TILE, KV_TILE + PAGE), 1)
    meta_ok = (col >= KV_TILE) & (col < KV_TILE + N_META)
    first_masks = [((col <= row + e * KV_TILE) & (col < KV_TILE)) | meta_ok for e in range(2)]
    block(lambda h: jnp.concatenate([k_a(h), km_ref[:, h * V_DIM:(h + 1) * V_DIM]], axis=0),
          lambda h: jnp.concatenate([v_a(h), vm_ref[:, h * V_DIM:(h + 1) * V_DIM]], axis=0),
          [(rows, first_masks[e]) for e in range(2) for rows in half_rows[e]], True)
    causal = (lax.broadcasted_iota(jnp.int32, (KV_TILE, KV_TILE), 1)
              <= lax.broadcasted_iota(jnp.int32, (KV_TILE, KV_TILE), 0))
    block(*kv_block(n_full + 1), [(rows, causal) for rows in half_rows[1]], False)

    rem = n_full % KV_UNROLL
    width = tiles_per_q
    while width < KV_UNROLL:
        def leftover(width=width):
            first_blk = (n_full // KV_UNROLL) * KV_UNROLL + (rem & (width - 1))
            for d in range(width):
                block(*kv_block(first_blk + d), all_rows, False)
        pl.when(rem & width != 0)(leftover)
        width *= 2

    def full_group(jj, carry):
        for d in range(KV_UNROLL):
            block(*kv_block(KV_UNROLL * jj + d), all_rows, False)
        return carry

    lax.fori_loop(0, n_full // KV_UNROLL, full_group, 0)

    lam = _lambda(lq1_ref, lk1_ref, lq2_ref, lk2_ref)
    for h in range(HEADS):
        acc = acc_sc[h]
        o = acc[:, :V_DIM] / acc[:, V_DIM:]
        o_ref[:, h * V_DIM:(h + 1) * V_DIM] = _head_out(o[:tq], o[tq:], lam, g_ref[...]).astype(o_ref.dtype)


def _prompt_attn_call(q, k, v, km, vm, lq1, lk1, lq2, lk2, subln_g):
    b, s, _ = q.shape
    small = lambda shape: pl.BlockSpec(shape, lambda b_, i: (0,) * len(shape))
    resident = pl.BlockSpec((None, s, ATTN_W), lambda b_, i: (b_, 0, 0), pipeline_mode=pl.Buffered(1))
    return pl.pallas_call(
        _prompt_attn_kernel,
        grid=(b, s // Q_TILE),
        in_specs=[pl.BlockSpec((None, Q_TILE, ATTN_W), lambda b_, i: (b_, i, 0)), resident, resident,
                  small((PAGE, ATTN_W)), small((PAGE, ATTN_W)),
                  small((1, HEAD_DIM)), small((1, HEAD_DIM)), small((1, HEAD_DIM)), small((1, HEAD_DIM)),
                  small((1, V_DIM))],
        out_specs=pl.BlockSpec((None, Q_TILE, ATTN_W), lambda b_, i: (b_, i, 0)),
        out_shape=jax.ShapeDtypeStruct((b, s, ATTN_W), BF16),
        scratch_shapes=[pltpu.VMEM((HEADS, 2 * Q_TILE, LANES), F32),
                        pltpu.VMEM((HEADS, 2 * Q_TILE, 2 * V_DIM), F32)],
        compiler_params=pltpu.CompilerParams(dimension_semantics=("arbitrary",) * 2,
                                             vmem_limit_bytes=VMEM_LIMIT),
        name="prompt_attn",
    )(q, k, v, km, vm, lq1, lk1, lq2, lk2, subln_g)


def _meta_attn_kernel(q_ref, k_ref, v_ref, lq1_ref, lk1_ref, lq2_ref, lk2_ref, g_ref, o_ref):
    t = q_ref.shape[0]
    qs = _stack_maps(q_ref[...].astype(BF16))
    pad = jnp.zeros((PAGE - t, V_DIM), F32)
    k = jnp.concatenate([k_ref[...], pad], axis=0).astype(BF16)
    v = jnp.concatenate([v_ref[...], pad], axis=0).astype(BF16)
    s = _dot_nt(qs, k)
    row = lax.broadcasted_iota(jnp.int32, s.shape, 0)
    col = lax.broadcasted_iota(jnp.int32, s.shape, 1)
    qrow = jnp.where(row >= t, row - t, row)
    s = jnp.where(col <= qrow, s, NEG_INF)
    p = jnp.exp(s - jnp.max(s, axis=-1, keepdims=True))
    o = _dot(p.astype(BF16), v) / jnp.sum(p, axis=-1, keepdims=True)
    lam = _lambda(lq1_ref, lk1_ref, lq2_ref, lk2_ref)
    o_ref[...] = _head_out(o[:t], o[t:], lam, g_ref[...])


def _meta_attn_call(q, k, v, lq1, lk1, lq2, lk2, subln_g):
    small = lambda shape: pl.BlockSpec(shape, lambda h: (0,) * len(shape))
    head = pl.BlockSpec((N_META, V_DIM), lambda h: (0, h))
    return pl.pallas_call(
        _meta_attn_kernel,
        grid=(HEADS,),
        in_specs=[head, head, head, small((1, HEAD_DIM)), small((1, HEAD_DIM)), small((1, HEAD_DIM)),
                  small((1, HEAD_DIM)), small((1, V_DIM))],
        out_specs=head,
        out_shape=jax.ShapeDtypeStruct((N_META, ATTN_W), F32),
        compiler_params=pltpu.CompilerParams(dimension_semantics=("arbitrary",)),
        name="meta_attn",
    )(q, k, v, lq1, lk1, lq2, lk2, subln_g)


def kernel(x_prompt, x_sample, cache_k, cache_v, state_conv, page_table, meta_tokens, ffn1_pre_g, ffn1_w_gate, ffn1_w_up, ffn1_w_down, ffn1_post_g, mix_pre_g, w_in, lambda_q1, lambda_k1, lambda_q2, lambda_k2, subln_g, conv_w, w_out, mix_post_g, ffn2_pre_g, ffn2_w_gate, ffn2_w_up, ffn2_w_down, ffn2_post_g):
    batch, seq, _ = x_prompt.shape
    n_seq, t_new, _ = x_sample.shape
    n_pool = cache_k.shape[1]
    l = 0

    bf = lambda w: w[l].astype(BF16)
    f1 = (ffn1_pre_g, bf(ffn1_w_gate), bf(ffn1_w_up), bf(ffn1_w_down), ffn1_post_g)
    f2 = (ffn2_pre_g, bf(ffn2_w_gate), bf(ffn2_w_up), bf(ffn2_w_down), ffn2_post_g)
    w_in_b, w_out_b = bf(w_in), bf(w_out)
    lam_args = (lambda_q1, lambda_k1, lambda_q2, lambda_k2, subln_g)
    ckt = jnp.transpose(cache_k[l], (0, 2, 3, 4, 1)).reshape(n_pool, ATTN_W, PAGE)
    cv = cache_v[l].reshape(n_pool, PAGE * HEADS, V_DIM)

    n_s = n_seq * t_new
    n_x = N_META + n_s
    xe = jnp.concatenate([meta_tokens, x_sample.reshape(n_s, D_MODEL)], axis=0)
    st = state_conv[l]
    zpad = lambda a, n: jnp.concatenate([a, jnp.zeros((n_seq, n, CONV_W), F32)], axis=1).reshape(n_s, CONV_W)
    zmeta = jnp.zeros((N_META, CONV_W), F32)
    p1 = jnp.concatenate([zmeta, zpad(st[:, 1:2], t_new - 1)], axis=0)
    p2 = jnp.concatenate([zmeta, zpad(st, t_new - 2)], axis=0)
    r = jnp.arange(n_x)[:, None]
    in_sample = r >= N_META
    pos = jnp.where(in_sample, (r - N_META) % t_new, r)
    m1 = jnp.broadcast_to((pos < 1).astype(F32), (n_x, CONV_W))
    m2 = jnp.broadcast_to((pos < 2).astype(F32), (n_x, CONV_W))

    xe1 = _ffn_stage_call(xe, f1, n_x, name="ffn1_extra")
    qe, ke, ve, ce, ue = _inproj_extra_call(xe1, mix_pre_g, w_in_b, conv_w[l], p1, p2, m1, m2)
    ae_meta = _meta_attn_call(qe[:N_META], ke[:N_META], ve[:N_META], *lam_args)

    rows = batch * seq
    half = n_seq // 2
    host = lambda seq0: (page_table, seq0, half, qe, ke, ve, N_META, t_new, ckt, cv, lam_args)
    xp = x_prompt.reshape(rows, D_MODEL)
    xp1, ae_s0 = _ffn_stage_call(xp, f1, HOST_ROW_TILE, host=host(0), name="ffn1")
    init = jnp.concatenate([jnp.zeros((SUBLANES - 2, CONV_W), F32), ue[N_META - 2:N_META]], axis=0)
    qp, kpt, vp4, kpb, vpb, cp, tail = _inproj_main_call(xp1, mix_pre_g, w_in_b, conv_w[l], init,
                                                        tile=ROW_TILE, rows_per_seq=seq)
    meta_pad = lambda a: jnp.concatenate([a[:N_META], jnp.zeros((PAGE - N_META, ATTN_W), F32)], axis=0).astype(BF16)
    shp = (batch, seq, ATTN_W)
    ap = _prompt_attn_call(qp.reshape(shp), kpb.reshape(shp), vpb.reshape(shp), meta_pad(ke), meta_pad(ve),
                           *lam_args)
    yp, ae_s1 = _ffn_stage_call(xp1, f2, HOST_ROW_TILE, mix=(ap.reshape(rows, ATTN_W), cp, w_out_b, mix_post_g),
                                host=host(half), name="mixout_ffn2")
    ae = jnp.concatenate([ae_meta, ae_s0, ae_s1], axis=0)
    ye = _ffn_stage_call(xe1, f2, n_x, mix=(ae, ce, w_out_b, mix_post_g), name="mixout_ffn2_extra")

    y_prompt = yp.reshape(batch, seq, D_MODEL)
    y_sample = ye[N_META:].reshape(n_seq, t_new, D_MODEL)
    kt_meta = jnp.broadcast_to(ke[:N_META].T[None], (batch, ATTN_W, N_META))
    kt_all = jnp.concatenate([kt_meta, kpt], axis=2).reshape(batch, HEADS, 2, HEAD_DIM, N_META + seq)
    k_prompt_new = jnp.transpose(kt_all, (0, 4, 1, 2, 3))[None]
    v_meta = jnp.broadcast_to(ve[:N_META].reshape(1, N_META, HEADS, V_DIM), (batch, N_META, HEADS, V_DIM))
    v_prompt_new = jnp.concatenate([v_meta, vp4.reshape(batch, seq, HEADS, V_DIM)], axis=1)[None]
    conv_prompt_new = tail[None]
    k_sample_new = ke[N_META:].reshape(1, n_seq, t_new, HEADS, 2, HEAD_DIM)
    v_sample_new = ve[N_META:].reshape(1, n_seq, t_new, HEADS, V_DIM)
    conv_sample_new = ue[N_META:].reshape(n_seq, t_new, CONV_W)[None, :, t_new - 2:, :]
    return (y_prompt, y_sample, k_prompt_new, v_prompt_new, conv_prompt_new,
            k_sample_new, v_sample_new, conv_sample_new)
```

```python
import functools
import math

import jax
import jax.numpy as jnp
from jax import lax
from jax.experimental import pallas as pl
from jax.experimental.pallas import tpu as pltpu

F32 = jnp.float32
BF16 = jnp.bfloat16

D_MODEL = 1024
D_FF = 2816
N_META = 16
HEADS = 4
HEAD_DIM = 64
V_DIM = 2 * HEAD_DIM
ATTN_W = HEADS * V_DIM
CONV_W = D_MODEL - ATTN_W
IN_COLS = 3 * ATTN_W + 3 * CONV_W
PAGE = 128
RMS_EPS = 1e-6
SUBLN_EPS = 1e-5
NEG_INF = -1e30
LAMBDA_INIT = 0.8 - 0.6 * math.exp(-0.3 * 0)
QK_SCALE = HEAD_DIM ** -0.5
LOG2E = math.log2(math.e)

SUBLANES = 8
LANES = 128
VMEM_LIMIT = 56 * 1024 * 1024
HOST_VMEM_LIMIT = 62 * 1024 * 1024

ROW_TILE = 512
HOST_ROW_TILE = 512
MXU_TILE = 256
FF_CHUNKS = ((0, 6 * MXU_TILE), (6 * MXU_TILE, D_FF))
KV_TILE = 256
Q_TILE = 2 * KV_TILE
KV_UNROLL = 4


def _rms(x, g, eps):
    ms = jnp.mean(x * x, axis=-1, keepdims=True)
    return x * lax.rsqrt(ms + eps) * g


def _dot(a, b):
    return jnp.dot(a, b, preferred_element_type=F32)


def _dot_nt(a, b):
    return lax.dot_general(a, b, (((1,), (1,)), ((), ())), preferred_element_type=F32)


def _ffn_half_step(x, pre_g, wg_ref, wu_ref, wd_ref, post_g, side_work=()):
    side_work = list(side_work)
    run_side = lambda: side_work.pop(0)() if side_work else None
    h = _rms(x, pre_g, RMS_EPS).astype(BF16)
    acc = None
    for lo, hi in FF_CHUNKS:
        sl = slice(lo, hi)
        g = _dot(h, wg_ref[:, sl])
        run_side()
        u = _dot(h, wu_ref[:, sl])
        run_side()
        a = (g * jax.nn.sigmoid(g) * u).astype(BF16)
        d = _dot(a, wd_ref[sl, :])
        run_side()
        acc = d if acc is None else acc + d
    assert not side_work
    return x + 0.5 * _rms(acc, post_g, RMS_EPS)


def _lambda(lq1_ref, lk1_ref, lq2_ref, lk2_ref):
    a = jnp.exp(jnp.sum(lq1_ref[...] * lk1_ref[...], axis=-1, keepdims=True))
    b = jnp.exp(jnp.sum(lq2_ref[...] * lk2_ref[...], axis=-1, keepdims=True))
    return a - b + LAMBDA_INIT


def _head_out(o1, o2, lam, g):
    o = o1 - lam * o2
    return _rms(o, g, SUBLN_EPS) * (1.0 - LAMBDA_INIT)


class _SampleAttn:
    N_REFS = 8

    def __init__(self, in_refs, k_refs, v_refs, o_ref, qbd_sc, m_sc, l_sc, acc_sc):
        self.q_ref, self.kn_ref, self.vn_ref = in_refs[:3]
        self.lam_refs, self.g_ref = in_refs[3:7], in_refs[7]
        self.k_refs, self.v_refs, self.o_ref = k_refs, v_refs, o_ref
        self.qbd_sc, self.m_sc, self.l_sc, self.acc_sc = qbd_sc, m_sc, l_sc, acc_sc
        self.t = self.q_ref.shape[0]

    def _softmax(self, s):
        m_prev = self.m_sc[...]
        m_new = jnp.maximum(m_prev, jnp.max(s, axis=1, keepdims=True))
        alpha = jnp.exp(m_prev - m_new)
        p = jnp.exp(s - m_new)
        self.l_sc[...] = alpha * self.l_sc[...] + jnp.sum(p, axis=1, keepdims=True)
        self.m_sc[...] = m_new
        return alpha, p.astype(BF16)

    def _values(self, g, alpha, pb, v_pair):
        t, acc_sc = self.t, self.acc_sc
        pv = _dot(pb[g * 4 * t:(g + 1) * 4 * t], v_pair)
        for e in range(2):
            rs = slice((2 * g + e) * 2 * t, (2 * g + e + 1) * 2 * t)
            acc_sc[rs, :] = (alpha[rs] * acc_sc[rs, :]
                             + pv[e * 2 * t:(e + 1) * 2 * t, e * V_DIM:(e + 1) * V_DIM])

    def start(self, j):
        t = self.t
        n_rows = HEADS * 2 * t

        @pl.when(j == 0)
        def _():
            qt = jnp.concatenate([self.q_ref[...]] * (n_rows // t), axis=0)
            row = lax.broadcasted_iota(jnp.int32, qt.shape, 0)
            col = lax.broadcasted_iota(jnp.int32, qt.shape, 1)
            self.qbd_sc[...] = jnp.where(col // HEAD_DIM == row // t, qt, 0.0).astype(BF16)
            self.m_sc[...] = jnp.full(self.m_sc.shape, NEG_INF, F32)
            self.l_sc[...] = jnp.zeros(self.l_sc.shape, F32)
            self.acc_sc[...] = jnp.zeros(self.acc_sc.shape, F32)

    def page_work(self, slots, kb_sc, vb_sc):
        st = {}

        def cast():
            for n, r in enumerate(slots):
                cols = slice(n * PAGE, (n + 1) * PAGE)
                kb_sc[:, cols] = self.k_refs[r][...].astype(BF16)
                for h in range(HEADS):
                    vb_sc[h // 2, cols, (h % 2) * V_DIM:(h % 2 + 1) * V_DIM] = (
                        self.v_refs[r][pl.ds(h, PAGE, stride=HEADS), :].astype(BF16))

        def scores():
            st["s"] = _dot(self.qbd_sc[...], kb_sc[...])

        def values():
            alpha, pb = self._softmax(st["s"])
            for g in range(HEADS // 2):
                self._values(g, alpha, pb, vb_sc[g])

        return cast, scores, values

    def finish(self, j, n_steps):
        t = self.t

        @pl.when(j == n_steps - 1)
        def _():
            pad = jnp.zeros((PAGE - t, ATTN_W), F32)
            kn = jnp.concatenate([self.kn_ref[...], pad], axis=0).astype(BF16)
            vn = jnp.concatenate([self.vn_ref[...], pad], axis=0).astype(BF16)
            s = _dot_nt(self.qbd_sc[...], kn)
            row = lax.broadcasted_iota(jnp.int32, s.shape, 0)
            key = lax.broadcasted_iota(jnp.int32, s.shape, 1)
            s = jnp.where((key < t) & (key <= row % t), s, NEG_INF)
            alpha, pb = self._softmax(s)
            for g in range(HEADS // 2):
                self._values(g, alpha, pb, vn[:, g * 2 * V_DIM:(g + 1) * 2 * V_DIM])
            o = self.acc_sc[...] / self.l_sc[...]
            lam = _lambda(*self.lam_refs)
            for h in range(HEADS):
                r0 = h * 2 * t
                self.o_ref[:, h * V_DIM:(h + 1) * V_DIM] = _head_out(
                    o[r0:r0 + t], o[r0 + t:r0 + 2 * t], lam, self.g_ref[...])


def _ffn_stage_kernel(has_mix, host, *refs):
    refs = list(refs)
    take = lambda n: [refs.pop(0) for _ in range(n)]
    if host:
        (pt_ref,) = take(1)
    (x_ref,) = take(1)
    if has_mix:
        a_ref, c_ref, wo_ref, mpost_ref = take(4)
    pre_ref, wg_ref, wu_ref, wd_ref, post_ref = take(5)
    if host:
        sample_refs = take(_SampleAttn.N_REFS)
        kt_hbm, v_hbm = take(2)
    (o_ref,) = take(1)
    side_work = ()
    if host:
        pps, host_steps, seq0, n_pages = host
        ao_ref, kbuf, vbuf, sems, kb_sc, vb_sc = take(6)
        i = pl.program_id(0)
        last = pl.num_programs(0) - 1
        j = i % host_steps
        halves = (range(0, pps // 2), range(pps // 2, pps))

        def copies(step, slots):
            first_page = (seq0 + step // host_steps) * n_pages + (step % host_steps) * pps
            out = []
            for r in slots:
                page = pt_ref[first_page + r]
                out.append(pltpu.make_async_copy(kt_hbm.at[page], kbuf.at[r], sems.at[r]))
                out.append(pltpu.make_async_copy(v_hbm.at[page], vbuf.at[r], sems.at[pps + r]))
            return out

        start = lambda cs: [c.start() for c in cs]
        wait = lambda cs: [c.wait() for c in cs]
        first_half, second_half = copies(i, halves[0]), copies(i, halves[1])
        next_first_half = copies(jnp.minimum(i + 1, last), halves[0])

        @pl.when(i == 0)
        def _():
            start(first_half)

        start(second_half)
        sample = _SampleAttn(sample_refs, [kbuf.at[r] for r in range(pps)], [vbuf.at[r] for r in range(pps)],
                             ao_ref, *refs)
        sample.start(j)
        cast_a, scores_a, values_a = sample.page_work(halves[0], kb_sc, vb_sc)
        cast_b, scores_b, values_b = sample.page_work(halves[1], kb_sc, vb_sc)
        wait(first_half)
        cast_a()
        side_work = [scores_a, values_a, lambda: (start(next_first_half), wait(second_half), cast_b()),
                     scores_b, values_b]
    x = x_ref[...]
    if has_mix:
        m = _dot(a_ref[...].astype(BF16), wo_ref[:ATTN_W, :]) + _dot(c_ref[...], wo_ref[ATTN_W:, :])
        x = x + _rms(m, mpost_ref[...], RMS_EPS)
    o_ref[...] = _ffn_half_step(x, pre_ref[...], wg_ref, wu_ref, wd_ref, post_ref[...], side_work)
    if host:
        sample.finish(j, host_steps)

        @pl.when(i == last)
        def _():
            wait(next_first_half)


def _const_spec(shape):
    return pl.BlockSpec(shape, lambda *_: (0,) * len(shape), pipeline_mode=pl.Buffered(1))


def _row_spec(tile, cols):
    return pl.BlockSpec((tile, cols), lambda i, *_: (i, 0))


def _ffn_weight_specs():
    return [_const_spec((1, D_MODEL)), _const_spec((D_MODEL, D_FF)), _const_spec((D_MODEL, D_FF)),
            _const_spec((D_FF, D_MODEL)), _const_spec((1, D_MODEL))]


def _row_params(vmem_limit=VMEM_LIMIT):
    return pltpu.CompilerParams(dimension_semantics=("arbitrary",), vmem_limit_bytes=vmem_limit)


def _ffn_stage_call(x, ffn, tile, mix=None, host=None, name="ffn"):
    rows = x.shape[0]
    steps = rows // tile
    args = [x]
    in_specs = [_row_spec(tile, D_MODEL)]
    if mix is not None:
        args += list(mix)
        in_specs += [_row_spec(tile, ATTN_W), _row_spec(tile, CONV_W),
                     _const_spec((D_MODEL, D_MODEL)), _const_spec((1, D_MODEL))]
    args += list(ffn)
    in_specs += _ffn_weight_specs()
    out_specs = _row_spec(tile, D_MODEL)
    out_shape = jax.ShapeDtypeStruct((rows, D_MODEL), F32)
    if host is None:
        return pl.pallas_call(
            functools.partial(_ffn_stage_kernel, mix is not None, None),
            grid=(steps,), in_specs=in_specs, out_specs=out_specs, out_shape=out_shape,
            compiler_params=_row_params(), name=name,
        )(*args)

    page_table, seq0, n_host_seq, q, k_new, v_new, row0, t, cache_kt, cache_v, lam_args = host
    n_pages = page_table.shape[1]
    assert steps % n_host_seq == 0 and row0 % t == 0
    host_steps = steps // n_host_seq
    pps = n_pages // host_steps
    assert pps * host_steps == n_pages
    blk0 = row0 // t + seq0
    n_rows = HEADS * 2 * t
    seq_spec = pl.BlockSpec((t, ATTN_W), lambda i, pt: (blk0 + i // host_steps, 0))
    small = lambda shape: pl.BlockSpec(shape, lambda i, pt: (0,) * len(shape))
    in_hbm = pl.BlockSpec(memory_space=pl.ANY)

    args += [q, k_new, v_new, *lam_args, cache_kt, cache_v]
    in_specs += [seq_spec] * 3 + [small((1, HEAD_DIM))] * 4 + [small((1, V_DIM))] + [in_hbm] * 2
    grid_spec = pltpu.PrefetchScalarGridSpec(
        num_scalar_prefetch=1,
        grid=(steps,),
        in_specs=in_specs,
        out_specs=[out_specs, pl.BlockSpec((t, ATTN_W), lambda i, pt: (i // host_steps, 0))],
        scratch_shapes=[pltpu.VMEM((pps, ATTN_W, PAGE), F32), pltpu.VMEM((pps, PAGE * HEADS, V_DIM), F32),
                        pltpu.SemaphoreType.DMA((2 * pps,)),
                        pltpu.VMEM((ATTN_W, pps // 2 * PAGE), BF16),
                        pltpu.VMEM((HEADS // 2, pps // 2 * PAGE, 2 * V_DIM), BF16),
                        pltpu.VMEM((n_rows, ATTN_W), BF16), pltpu.VMEM((n_rows, 1), F32),
                        pltpu.VMEM((n_rows, 1), F32), pltpu.VMEM((n_rows, V_DIM), F32)],
    )
    return pl.pallas_call(
        functools.partial(_ffn_stage_kernel, mix is not None, (pps, host_steps, seq0, n_pages)),
        grid_spec=grid_spec,
        out_shape=[out_shape, jax.ShapeDtypeStruct((n_host_seq * t, ATTN_W), F32)],
        compiler_params=_row_params(HOST_VMEM_LIMIT), name=name,
    )(page_table.reshape(-1), *args)


def _in_projection(x, g, w_ref, q_scale):
    h = _rms(x, g, RMS_EPS).astype(BF16)
    p = _dot(h, w_ref[...])
    q = p[:, :ATTN_W] * q_scale
    k = p[:, ATTN_W:2 * ATTN_W]
    v = p[:, 2 * ATTN_W:3 * ATTN_W]
    o = 3 * ATTN_W
    gate_b = p[:, o:o + CONV_W]
    u = p[:, o + CONV_W:o + 2 * CONV_W] * p[:, o + 2 * CONV_W:o + 3 * CONV_W]
    return q, k, v, gate_b, u


def _inproj_main_kernel(tiles_per_seq, x_ref, g_ref, w_ref, cw_ref, init_ref,
                        q_ref, kt_ref, v4_ref, kb_ref, vb_ref, c_ref, tail_ref, ubuf):
    tile = x_ref.shape[0]

    @pl.when(pl.program_id(0) % tiles_per_seq == 0)
    def _():
        ubuf[0:SUBLANES, :] = init_ref[...]

    q, k, v, gate_b, u = _in_projection(x_ref[...], g_ref[...], w_ref, QK_SCALE * LOG2E)
    q_ref[...] = q.astype(BF16)
    kt_ref[...] = k.T
    for h in range(HEADS):
        v4_ref[pl.ds(h, tile, stride=HEADS), :] = v[:, h * V_DIM:(h + 1) * V_DIM]
    kb_ref[...] = k.astype(BF16)
    vb_ref[...] = v.astype(BF16)
    ubuf[SUBLANES:SUBLANES + tile, :] = u
    u1 = ubuf[SUBLANES - 1:SUBLANES - 1 + tile, :]
    u2 = ubuf[SUBLANES - 2:SUBLANES - 2 + tile, :]
    y = cw_ref[0:1, :] * u2 + cw_ref[1:2, :] * u1 + cw_ref[2:3, :] * u
    c_ref[...] = (gate_b * y).astype(BF16)
    tail_ref[...] = u[tile - 2:, :]
    ubuf[0:SUBLANES, :] = u[tile - SUBLANES:, :]


def _inproj_extra_kernel(x_ref, g_ref, w_ref, cw_ref, p1_ref, p2_ref, m1_ref, m2_ref,
                         q_ref, kf_ref, vf_ref, c_ref, u_ref, ubuf):
    tile = x_ref.shape[0]
    q, k, v, gate_b, u = _in_projection(x_ref[...], g_ref[...], w_ref, QK_SCALE)
    q_ref[...] = q
    kf_ref[...] = k
    vf_ref[...] = v
    u_ref[...] = u
    ubuf[0:SUBLANES, :] = jnp.zeros((SUBLANES, CONV_W), F32)
    ubuf[SUBLANES:SUBLANES + tile, :] = u
    u1 = jnp.where(m1_ref[...] > 0.5, p1_ref[...], ubuf[SUBLANES - 1:SUBLANES - 1 + tile, :])
    u2 = jnp.where(m2_ref[...] > 0.5, p2_ref[...], ubuf[SUBLANES - 2:SUBLANES - 2 + tile, :])
    y = cw_ref[0:1, :] * u2 + cw_ref[1:2, :] * u1 + cw_ref[2:3, :] * u
    c_ref[...] = (gate_b * y).astype(BF16)


def _inproj_main_call(x1, g, w_in, conv_w, init, tile, rows_per_seq):
    rows = x1.shape[0]
    tiles_per_seq = rows_per_seq // tile
    n_seq = rows // rows_per_seq
    sd = jax.ShapeDtypeStruct
    return pl.pallas_call(
        functools.partial(_inproj_main_kernel, tiles_per_seq),
        grid=(rows // tile,),
        in_specs=[_row_spec(tile, D_MODEL), _const_spec((1, D_MODEL)), _const_spec((D_MODEL, IN_COLS)),
                  _const_spec((3, CONV_W)), _const_spec((SUBLANES, CONV_W))],
        out_specs=[_row_spec(tile, ATTN_W),
                   pl.BlockSpec((None, ATTN_W, tile), lambda i: (i // tiles_per_seq, 0, i % tiles_per_seq)),
                   _row_spec(tile * HEADS, V_DIM),
                   _row_spec(tile, ATTN_W), _row_spec(tile, ATTN_W), _row_spec(tile, CONV_W),
                   pl.BlockSpec((None, 2, CONV_W), lambda i: (i // tiles_per_seq, 0, 0))],
        out_shape=[sd((rows, ATTN_W), BF16), sd((n_seq, ATTN_W, rows_per_seq), F32),
                   sd((rows * HEADS, V_DIM), F32),
                   sd((rows, ATTN_W), BF16), sd((rows, ATTN_W), BF16), sd((rows, CONV_W), BF16),
                   sd((n_seq, 2, CONV_W), F32)],
        scratch_shapes=[pltpu.VMEM((SUBLANES + tile, CONV_W), F32)],
        compiler_params=_row_params(),
        name="inproj_main",
    )(x1, g, w_in, conv_w, init)


def _inproj_extra_call(x1, g, w_in, conv_w, p1, p2, m1, m2):
    rows = x1.shape[0]
    sd = jax.ShapeDtypeStruct
    full = lambda cols: pl.BlockSpec((rows, cols), lambda i: (0, 0))
    return pl.pallas_call(
        _inproj_extra_kernel,
        grid=(1,),
        in_specs=[full(D_MODEL), _const_spec((1, D_MODEL)), _const_spec((D_MODEL, IN_COLS)),
                  _const_spec((3, CONV_W))] + [full(CONV_W)] * 4,
        out_specs=[full(ATTN_W)] * 3 + [full(CONV_W)] * 2,
        out_shape=[sd((rows, ATTN_W), F32)] * 3 + [sd((rows, CONV_W), BF16), sd((rows, CONV_W), F32)],
        scratch_shapes=[pltpu.VMEM((SUBLANES + rows, CONV_W), F32)],
        compiler_params=_row_params(),
        name="inproj_extra",
    )(x1, g, w_in, conv_w, p1, p2, m1, m2)


def _stack_maps(q):
    lane = lax.broadcasted_iota(jnp.int32, q.shape, 1)
    zero = jnp.zeros_like(q)
    return jnp.concatenate([jnp.where(lane < HEAD_DIM, q, zero), jnp.where(lane >= HEAD_DIM, q, zero)], axis=0)


def _prompt_attn_kernel(q_ref, k_ref, v_ref, km_ref, vm_ref, lq1_ref, lk1_ref, lq2_ref, lk2_ref, g_ref,
                        o_ref, m_sc, acc_sc):
    tq = q_ref.shape[0]
    i = pl.program_id(1)
    qs = [_stack_maps(q_ref[:, h * V_DIM:(h + 1) * V_DIM]) for h in range(HEADS)]
    all_rows = [(slice(0, 2 * tq), None)]
    half_rows = [tuple(slice(m * tq + e * KV_TILE, m * tq + (e + 1) * KV_TILE) for m in range(2)) for e in range(2)]

    def update(h, rows, s, v_aug, first):
        m_cur = jnp.max(s, axis=1, keepdims=True)
        if first:
            m_new = jnp.broadcast_to(m_cur, (s.shape[0], LANES))
            p = jnp.exp2(s - m_cur)
            acc_sc[h, rows, :] = _dot(p.astype(BF16), v_aug)
        else:
            m_prev = m_sc[h, rows, :]
            m_new = jnp.maximum(m_prev, m_cur)
            alpha = jnp.exp2(m_prev - m_new)
            reps = s.shape[1] // LANES
            p = jnp.exp2(s - jnp.concatenate([m_new] * reps, axis=1))
            acc_sc[h, rows, :] = (jnp.concatenate([alpha, alpha], axis=1) * acc_sc[h, rows, :]
                                  + _dot(p.astype(BF16), v_aug))
        m_sc[h, rows, :] = m_new

    def block(k_blk, v_blk, row_groups, first):
        for h in range(HEADS):
            k = k_blk(h)
            v_aug = jnp.concatenate([v_blk(h), jnp.ones(k.shape, BF16)], axis=1)
            for rows, mask in row_groups:
                s = _dot_nt(qs[h][rows], k)
                if mask is not None:
                    s = jnp.where(mask, s, NEG_INF)
                update(h, rows, s, v_aug, first)

    def kv_block(j):
        start = pl.multiple_of(j * KV_TILE, KV_TILE)
        return (lambda h: k_ref[pl.ds(start, KV_TILE), h * V_DIM:(h + 1) * V_DIM],
                lambda h: v_ref[pl.ds(start, KV_TILE), h * V_DIM:(h + 1) * V_DIM])

    tiles_per_q = tq // KV_TILE
    n_full = i * tiles_per_q
    k_a, v_a = kv_block(n_full)
    row = lax.broadcasted_iota(jnp.int32, (KV_TILE, KV_TILE + PAGE), 0)
    col = lax.broadcasted_iota(jnp.int32, (KV_TILE, KV_TILE + PAGE), 1)
    meta_ok = (col >= KV_TILE) & (col < KV_TILE + N_META)
    first_masks = [((col <= row + e * KV_TILE) & (col < KV_TILE)) | meta_ok for e in range(2)]
    block(lambda h: jnp.concatenate([k_a(h), km_ref[:, h * V_DIM:(h + 1) * V_DIM]], axis=0),
          lambda h: jnp.concatenate([v_a(h), vm_ref[:, h * V_DIM:(h + 1) * V_DIM]], axis=0),
          [(rows, first_masks[e]) for e in range(2) for rows in half_rows[e]], True)
    causal = (lax.broadcasted_iota(jnp.int32, (KV_TILE, KV_TILE), 1)
              <= lax.broadcasted_iota(jnp.int32, (KV_TILE, KV_TILE), 0))
    block(*kv_block(n_full + 1), [(rows, causal) for rows in half_rows[1]], False)

    rem = n_full % KV_UNROLL
    width = tiles_per_q
    while width < KV_UNROLL:
        def leftover(width=width):
            first_blk = (n_full // KV_UNROLL) * KV_UNROLL + (rem & (width - 1))
            for d in range(width):
                block(*kv_block(first_blk + d), all_rows, False)
        pl.when(rem & width != 0)(leftover)
        width *= 2

    def full_group(jj, carry):
        for d in range(KV_UNROLL):
            block(*kv_block(KV_UNROLL * jj + d), all_rows, False)
        return carry

    lax.fori_loop(0, n_full // KV_UNROLL, full_group, 0)

    lam = _lambda(lq1_ref, lk1_ref, lq2_ref, lk2_ref)
    for h in range(HEADS):
        acc = acc_sc[h]
        o = acc[:, :V_DIM] / acc[:, V_DIM:]
        o_ref[:, h * V_DIM:(h + 1) * V_DIM] = _head_out(o[:tq], o[tq:], lam, g_ref[...]).astype(o_ref.dtype)


def _prompt_attn_call(q, k, v, km, vm, lq1, lk1, lq2, lk2, subln_g):
    b, s, _ = q.shape
    small = lambda shape: pl.BlockSpec(shape, lambda b_, i: (0,) * len(shape))
    resident = pl.BlockSpec((None, s, ATTN_W), lambda b_, i: (b_, 0, 0), pipeline_mode=pl.Buffered(1))
    return pl.pallas_call(
        _prompt_attn_kernel,
        grid=(b, s // Q_TILE),
        in_specs=[pl.BlockSpec((None, Q_TILE, ATTN_W), lambda b_, i: (b_, i, 0)), resident, resident,
                  small((PAGE, ATTN_W)), small((PAGE, ATTN_W)),
                  small((1, HEAD_DIM)), small((1, HEAD_DIM)), small((1, HEAD_DIM)), small((1, HEAD_DIM)),
                  small((1, V_DIM))],
        out_specs=pl.BlockSpec((None, Q_TILE, ATTN_W), lambda b_, i: (b_, i, 0)),
        out_shape=jax.ShapeDtypeStruct((b, s, ATTN_W), BF16),
        scratch_shapes=[pltpu.VMEM((HEADS, 2 * Q_TILE, LANES), F32),
                        pltpu.VMEM((HEADS, 2 * Q_TILE, 2 * V_DIM), F32)],
        compiler_params=pltpu.CompilerParams(dimension_semantics=("arbitrary",) * 2,
                                             vmem_limit_bytes=VMEM_LIMIT),
        name="prompt_attn",
    )(q, k, v, km, vm, lq1, lk1, lq2, lk2, subln_g)


def _meta_attn_kernel(q_ref, k_ref, v_ref, lq1_ref, lk1_ref, lq2_ref, lk2_ref, g_ref, o_ref):
    t = q_ref.shape[0]
    qs = _stack_maps(q_ref[...].astype(BF16))
    pad = jnp.zeros((PAGE - t, V_DIM), F32)
    k = jnp.concatenate([k_ref[...], pad], axis=0).astype(BF16)
    v = jnp.concatenate([v_ref[...], pad], axis=0).astype(BF16)
    s = _dot_nt(qs, k)
    row = lax.broadcasted_iota(jnp.int32, s.shape, 0)
    col = lax.broadcasted_iota(jnp.int32, s.shape, 1)
    qrow = jnp.where(row >= t, row - t, row)
    s = jnp.where(col <= qrow, s, NEG_INF)
    p = jnp.exp(s - jnp.max(s, axis=-1, keepdims=True))
    o = _dot(p.astype(BF16), v) / jnp.sum(p, axis=-1, keepdims=True)
    lam = _lambda(lq1_ref, lk1_ref, lq2_ref, lk2_ref)
    o_ref[...] = _head_out(o[:t], o[t:], lam, g_ref[...])


def _meta_attn_call(q, k, v, lq1, lk1, lq2, lk2, subln_g):
    small = lambda shape: pl.BlockSpec(shape, lambda h: (0,) * len(shape))
    head = pl.BlockSpec((N_META, V_DIM), lambda h: (0, h))
    return pl.pallas_call(
        _meta_attn_kernel,
        grid=(HEADS,),
        in_specs=[head, head, head, small((1, HEAD_DIM)), small((1, HEAD_DIM)), small((1, HEAD_DIM)),
                  small((1, HEAD_DIM)), small((1, V_DIM))],
        out_specs=head,
        out_shape=jax.ShapeDtypeStruct((N_META, ATTN_W), F32),
        compiler_params=pltpu.CompilerParams(dimension_semantics=("arbitrary",)),
        name="meta_attn",
    )(q, k, v, lq1, lk1, lq2, lk2, subln_g)


def kernel(x_prompt, x_sample, cache_k, cache_v, state_conv, page_table, meta_tokens, ffn1_pre_g, ffn1_w_gate, ffn1_w_up, ffn1_w_down, ffn1_post_g, mix_pre_g, w_in, lambda_q1, lambda_k1, lambda_q2, lambda_k2, subln_g, conv_w, w_out, mix_post_g, ffn2_pre_g, ffn2_w_gate, ffn2_w_up, ffn2_w_down, ffn2_post_g):
    batch, seq, _ = x_prompt.shape
    n_seq, t_new, _ = x_sample.shape
    n_pool = cache_k.shape[1]
    l = 0

    bf = lambda w: w[l].astype(BF16)
    f1 = (ffn1_pre_g, bf(ffn1_w_gate), bf(ffn1_w_up), bf(ffn1_w_down), ffn1_post_g)
    f2 = (ffn2_pre_g, bf(ffn2_w_gate), bf(ffn2_w_up), bf(ffn2_w_down), ffn2_post_g)
    w_in_b, w_out_b = bf(w_in), bf(w_out)
    lam_args = (lambda_q1, lambda_k1, lambda_q2, lambda_k2, subln_g)
    ckt = jnp.transpose(cache_k[l], (0, 2, 3, 4, 1)).reshape(n_pool, ATTN_W, PAGE)
    cv = cache_v[l].reshape(n_pool, PAGE * HEADS, V_DIM)

    n_s = n_seq * t_new
    n_x = N_META + n_s
    xe = jnp.concatenate([meta_tokens, x_sample.reshape(n_s, D_MODEL)], axis=0)
    st = state_conv[l]
    zpad = lambda a, n: jnp.concatenate([a, jnp.zeros((n_seq, n, CONV_W), F32)], axis=1).reshape(n_s, CONV_W)
    zmeta = jnp.zeros((N_META, CONV_W), F32)
    p1 = jnp.concatenate([zmeta, zpad(st[:, 1:2], t_new - 1)], axis=0)
    p2 = jnp.concatenate([zmeta, zpad(st, t_new - 2)], axis=0)
    r = jnp.arange(n_x)[:, None]
    in_sample = r >= N_META
    pos = jnp.where(in_sample, (r - N_META) % t_new, r)
    m1 = jnp.broadcast_to((pos < 1).astype(F32), (n_x, CONV_W))
    m2 = jnp.broadcast_to((pos < 2).astype(F32), (n_x, CONV_W))

    xe1 = _ffn_stage_call(xe, f1, n_x, name="ffn1_extra")
    qe, ke, ve, ce, ue = _inproj_extra_call(xe1, mix_pre_g, w_in_b, conv_w[l], p1, p2, m1, m2)
    ae_meta = _meta_attn_call(qe[:N_META], ke[:N_META], ve[:N_META], *lam_args)

    rows = batch * seq
    half = n_seq // 2
    host = lambda seq0: (page_table, seq0, half, qe, ke, ve, N_META, t_new, ckt, cv, lam_args)
    xp = x_prompt.reshape(rows, D_MODEL)
    xp1, ae_s0 = _ffn_stage_call(xp, f1, HOST_ROW_TILE, host=host(0), name="ffn1")
    init = jnp.concatenate([jnp.zeros((SUBLANES - 2, CONV_W), F32), ue[N_META - 2:N_META]], axis=0)
    qp, kpt, vp4, kpb, vpb, cp, tail = _inproj_main_call(xp1, mix_pre_g, w_in_b, conv_w[l], init,
                                                        tile=ROW_TILE, rows_per_seq=seq)
    meta_pad = lambda a: jnp.concatenate([a[:N_META], jnp.zeros((PAGE - N_META, ATTN_W), F32)], axis=0).astype(BF16)
    shp = (batch, seq, ATTN_W)
    ap = _prompt_attn_call(qp.reshape(shp), kpb.reshape(shp), vpb.reshape(shp), meta_pad(ke), meta_pad(ve),
                           *lam_args)
    yp, ae_s1 = _ffn_stage_call(xp1, f2, HOST_ROW_TILE, mix=(ap.reshape(rows, ATTN_W), cp, w_out_b, mix_post_g),
                                host=host(half), name="mixout_ffn2")
    ae = jnp.concatenate([ae_meta, ae_s0, ae_s1], axis=0)
    ye = _ffn_stage_call(xe1, f2, n_x, mix=(ae, ce, w_out_b, mix_post_g), name="mixout_ffn2_extra")

    y_prompt = yp.reshape(batch, seq, D_MODEL)
    y_sample = ye[N_META:].reshape(n_seq, t_new, D_MODEL)
    kt_meta = jnp.broadcast_to(ke[:N_META].T[None], (batch, ATTN_W, N_META))
    kt_all = jnp.concatenate([kt_meta, kpt], axis=2).reshape(batch, HEADS, 2, HEAD_DIM, N_META + seq)
    k_prompt_new = jnp.transpose(kt_all, (0, 4, 1, 2, 3))[None]
    v_meta = jnp.broadcast_to(ve[:N_META].reshape(1, N_META, HEADS, V_DIM), (batch, N_META, HEADS, V_DIM))
    v_prompt_new = jnp.concatenate([v_meta, vp4.reshape(batch, seq, HEADS, V_DIM)], axis=1)[None]
    conv_prompt_new = tail[None]
    k_sample_new = ke[N_META:].reshape(1, n_seq, t_new, HEADS, 2, HEAD_DIM)
    v_sample_new = ve[N_META:].reshape(1, n_seq, t_new, HEADS, V_DIM)
    conv_sample_new = ue[N_META:].reshape(n_seq, t_new, CONV_W)[None, :, t_new - 2:, :]
    return (y_prompt, y_sample, k_prompt_new, v_prompt_new, conv_prompt_new,
            k_sample_new, v_sample_new, conv_sample_new)
```

```python
import functools
import math

import jax
import jax.numpy as jnp
from jax import lax
from jax.experimental import pallas as pl
from jax.experimental.pallas import tpu as pltpu

F32 = jnp.float32
BF16 = jnp.bfloat16

D_MODEL = 1024
D_FF = 2816
N_META = 16
HEADS = 4
HEAD_DIM = 64
V_DIM = 2 * HEAD_DIM
ATTN_W = HEADS * V_DIM
CONV_W = D_MODEL - ATTN_W
IN_COLS = 3 * ATTN_W + 3 * CONV_W
PAGE = 128
RMS_EPS = 1e-6
SUBLN_EPS = 1e-5
NEG_INF = -1e30
LAMBDA_INIT = 0.8 - 0.6 * math.exp(-0.3 * 0)
QK_SCALE = HEAD_DIM ** -0.5
LOG2E = math.log2(math.e)

SUBLANES = 8
LANES = 128
VMEM_LIMIT = 56 * 1024 * 1024
HOST_VMEM_LIMIT = 62 * 1024 * 1024

ROW_TILE = 512
HOST_ROW_TILE = 512
MXU_TILE = 256
FF_CHUNKS = ((0, 6 * MXU_TILE), (6 * MXU_TILE, D_FF))
KV_TILE = 256
Q_TILE = 2 * KV_TILE
KV_UNROLL = 4


def _rms(x, g, eps):
    ms = jnp.mean(x * x, axis=-1, keepdims=True)
    return x * lax.rsqrt(ms + eps) * g


def _dot(a, b):
    return jnp.dot(a, b, preferred_element_type=F32)


def _dot_nt(a, b):
    return lax.dot_general(a, b, (((1,), (1,)), ((), ())), preferred_element_type=F32)


def _ffn_half_step(x, pre_g, wg_ref, wu_ref, wd_ref, post_g, side_work=()):
    side_work = list(side_work)
    run_side = lambda: side_work.pop(0)() if side_work else None
    h = _rms(x, pre_g, RMS_EPS).astype(BF16)
    acc = None
    for lo, hi in FF_CHUNKS:
        sl = slice(lo, hi)
        g = _dot(h, wg_ref[:, sl])
        run_side()
        u = _dot(h, wu_ref[:, sl])
        run_side()
        a = (g * jax.nn.sigmoid(g) * u).astype(BF16)
        d = _dot(a, wd_ref[sl, :])
        run_side()
        acc = d if acc is None else acc + d
    assert not side_work
    return x + 0.5 * _rms(acc, post_g, RMS_EPS)


def _lambda(lq1_ref, lk1_ref, lq2_ref, lk2_ref):
    a = jnp.exp(jnp.sum(lq1_ref[...] * lk1_ref[...], axis=-1, keepdims=True))
    b = jnp.exp(jnp.sum(lq2_ref[...] * lk2_ref[...], axis=-1, keepdims=True))
    return a - b + LAMBDA_INIT


def _head_out(o1, o2, lam, g):
    o = o1 - lam * o2
    return _rms(o, g, SUBLN_EPS) * (1.0 - LAMBDA_INIT)


class _SampleAttn:
    N_REFS = 8

    def __init__(self, in_refs, k_refs, v_refs, o_ref, qbd_sc, m_sc, l_sc, acc_sc):
        self.q_ref, self.kn_ref, self.vn_ref = in_refs[:3]
        self.lam_refs, self.g_ref = in_refs[3:7], in_refs[7]
        self.k_refs, self.v_refs, self.o_ref = k_refs, v_refs, o_ref
        self.qbd_sc, self.m_sc, self.l_sc, self.acc_sc = qbd_sc, m_sc, l_sc, acc_sc
        self.t = self.q_ref.shape[0]

    def _softmax(self, s):
        m_prev = self.m_sc[...]
        m_new = jnp.maximum(m_prev, jnp.max(s, axis=1, keepdims=True))
        alpha = jnp.exp(m_prev - m_new)
        p = jnp.exp(s - m_new)
        self.l_sc[...] = alpha * self.l_sc[...] + jnp.sum(p, axis=1, keepdims=True)
        self.m_sc[...] = m_new
        return alpha, p.astype(BF16)

    def _values(self, g, alpha, pb, v_pair):
        t, acc_sc = self.t, self.acc_sc
        pv = _dot(pb[g * 4 * t:(g + 1) * 4 * t], v_pair)
        for e in range(2):
            rs = slice((2 * g + e) * 2 * t, (2 * g + e + 1) * 2 * t)
            acc_sc[rs, :] = (alpha[rs] * acc_sc[rs, :]
                             + pv[e * 2 * t:(e + 1) * 2 * t, e * V_DIM:(e + 1) * V_DIM])

    def start(self, j):
        t = self.t
        n_rows = HEADS * 2 * t

        @pl.when(j == 0)
        def _():
            qt = jnp.concatenate([self.q_ref[...]] * (n_rows // t), axis=0)
            row = lax.broadcasted_iota(jnp.int32, qt.shape, 0)
            col = lax.broadcasted_iota(jnp.int32, qt.shape, 1)
            self.qbd_sc[...] = jnp.where(col // HEAD_DIM == row // t, qt, 0.0).astype(BF16)
            self.m_sc[...] = jnp.full(self.m_sc.shape, NEG_INF, F32)
            self.l_sc[...] = jnp.zeros(self.l_sc.shape, F32)
            self.acc_sc[...] = jnp.zeros(self.acc_sc.shape, F32)

    def page_work(self, slots, kb_sc, vb_sc):
        st = {}

        def cast():
            for n, r in enumerate(slots):
                cols = slice(n * PAGE, (n + 1) * PAGE)
                kb_sc[:, cols] = self.k_refs[r][...].astype(BF16)
                for h in range(HEADS):
                    vb_sc[h // 2, cols, (h % 2) * V_DIM:(h % 2 + 1) * V_DIM] = (
                        self.v_refs[r][pl.ds(h, PAGE, stride=HEADS), :].astype(BF16))

        def scores():
            st["s"] = _dot(self.qbd_sc[...], kb_sc[...])

        def values():
            alpha, pb = self._softmax(st["s"])
            for g in range(HEADS // 2):
                self._values(g, alpha, pb, vb_sc[g])

        return cast, scores, values

    def finish(self, j, n_steps):
        t = self.t

        @pl.when(j == n_steps - 1)
        def _():
            pad = jnp.zeros((PAGE - t, ATTN_W), F32)
            kn = jnp.concatenate([self.kn_ref[...], pad], axis=0).astype(BF16)
            vn = jnp.concatenate([self.vn_ref[...], pad], axis=0).astype(BF16)
            s = _dot_nt(self.qbd_sc[...], kn)
            row = lax.broadcasted_iota(jnp.int32, s.shape, 0)
            key = lax.broadcasted_iota(jnp.int32, s.shape, 1)
            s = jnp.where((key < t) & (key <= row % t), s, NEG_INF)
            alpha, pb = self._softmax(s)
            for g in range(HEADS // 2):
                self._values(g, alpha, pb, vn[:, g * 2 * V_DIM:(g + 1) * 2 * V_DIM])
            o = self.acc_sc[...] / self.l_sc[...]
            lam = _lambda(*self.lam_refs)
            for h in range(HEADS):
                r0 = h * 2 * t
                self.o_ref[:, h * V_DIM:(h + 1) * V_DIM] = _head_out(
                    o[r0:r0 + t], o[r0 + t:r0 + 2 * t], lam, self.g_ref[...])


def _ffn_stage_kernel(has_mix, host, *refs):
    refs = list(refs)
    take = lambda n: [refs.pop(0) for _ in range(n)]
    if host:
        (pt_ref,) = take(1)
    (x_ref,) = take(1)
    if has_mix:
        a_ref, c_ref, wo_ref, mpost_ref = take(4)
    pre_ref, wg_ref, wu_ref, wd_ref, post_ref = take(5)
    if host:
        sample_refs = take(_SampleAttn.N_REFS)
        kt_hbm, v_hbm = take(2)
    (o_ref,) = take(1)
    side_work = ()
    if host:
        pps, host_steps, seq0, n_pages = host
        ao_ref, kbuf, vbuf, sems, kb_sc, vb_sc = take(6)
        i = pl.program_id(0)
        last = pl.num_programs(0) - 1
        j = i % host_steps
        halves = (range(0, pps // 2), range(pps // 2, pps))

        def copies(step, slots):
            first_page = (seq0 + step // host_steps) * n_pages + (step % host_steps) * pps
            out = []
            for r in slots:
                page = pt_ref[first_page + r]
                out.append(pltpu.make_async_copy(kt_hbm.at[page], kbuf.at[r], sems.at[r]))
                out.append(pltpu.make_async_copy(v_hbm.at[page], vbuf.at[r], sems.at[pps + r]))
            return out

        start = lambda cs: [c.start() for c in cs]
        wait = lambda cs: [c.wait() for c in cs]
        first_half, second_half = copies(i, halves[0]), copies(i, halves[1])
        next_first_half = copies(jnp.minimum(i + 1, last), halves[0])

        @pl.when(i == 0)
        def _():
            start(first_half)

        start(second_half)
        sample = _SampleAttn(sample_refs, [kbuf.at[r] for r in range(pps)], [vbuf.at[r] for r in range(pps)],
                             ao_ref, *refs)
        sample.start(j)
        cast_a, scores_a, values_a = sample.page_work(halves[0], kb_sc, vb_sc)
        cast_b, scores_b, values_b = sample.page_work(halves[1], kb_sc, vb_sc)
        wait(first_half)
        cast_a()
        start(next_first_half)
        side_work = [scores_a, values_a, lambda: (wait(second_half), cast_b()), scores_b, values_b]
    x = x_ref[...]
    if has_mix:
        m = _dot(a_ref[...].astype(BF16), wo_ref[:ATTN_W, :]) + _dot(c_ref[...], wo_ref[ATTN_W:, :])
        x = x + _rms(m, mpost_ref[...], RMS_EPS)
    o_ref[...] = _ffn_half_step(x, pre_ref[...], wg_ref, wu_ref, wd_ref, post_ref[...], side_work)
    if host:
        sample.finish(j, host_steps)

        @pl.when(i == last)
        def _():
            wait(next_first_half)


def _const_spec(shape):
    return pl.BlockSpec(shape, lambda *_: (0,) * len(shape), pipeline_mode=pl.Buffered(1))


def _row_spec(tile, cols):
    return pl.BlockSpec((tile, cols), lambda i, *_: (i, 0))


def _ffn_weight_specs():
    return [_const_spec((1, D_MODEL)), _const_spec((D_MODEL, D_FF)), _const_spec((D_MODEL, D_FF)),
            _const_spec((D_FF, D_MODEL)), _const_spec((1, D_MODEL))]


def _row_params(vmem_limit=VMEM_LIMIT):
    return pltpu.CompilerParams(dimension_semantics=("arbitrary",), vmem_limit_bytes=vmem_limit)


def _ffn_stage_call(x, ffn, tile, mix=None, host=None, name="ffn"):
    rows = x.shape[0]
    steps = rows // tile
    args = [x]
    in_specs = [_row_spec(tile, D_MODEL)]
    if mix is not None:
        args += list(mix)
        in_specs += [_row_spec(tile, ATTN_W), _row_spec(tile, CONV_W),
                     _const_spec((D_MODEL, D_MODEL)), _const_spec((1, D_MODEL))]
    args += list(ffn)
    in_specs += _ffn_weight_specs()
    out_specs = _row_spec(tile, D_MODEL)
    out_shape = jax.ShapeDtypeStruct((rows, D_MODEL), F32)
    if host is None:
        return pl.pallas_call(
            functools.partial(_ffn_stage_kernel, mix is not None, None),
            grid=(steps,), in_specs=in_specs, out_specs=out_specs, out_shape=out_shape,
            compiler_params=_row_params(), name=name,
        )(*args)

    page_table, seq0, n_host_seq, q, k_new, v_new, row0, t, cache_kt, cache_v, lam_args = host
    n_pages = page_table.shape[1]
    assert steps % n_host_seq == 0 and row0 % t == 0
    host_steps = steps // n_host_seq
    pps = n_pages // host_steps
    assert pps * host_steps == n_pages
    blk0 = row0 // t + seq0
    n_rows = HEADS * 2 * t
    seq_spec = pl.BlockSpec((t, ATTN_W), lambda i, pt: (blk0 + i // host_steps, 0))
    small = lambda shape: pl.BlockSpec(shape, lambda i, pt: (0,) * len(shape))
    in_hbm = pl.BlockSpec(memory_space=pl.ANY)

    args += [q, k_new, v_new, *lam_args, cache_kt, cache_v]
    in_specs += [seq_spec] * 3 + [small((1, HEAD_DIM))] * 4 + [small((1, V_DIM))] + [in_hbm] * 2
    grid_spec = pltpu.PrefetchScalarGridSpec(
        num_scalar_prefetch=1,
        grid=(steps,),
        in_specs=in_specs,
        out_specs=[out_specs, pl.BlockSpec((t, ATTN_W), lambda i, pt: (i // host_steps, 0))],
        scratch_shapes=[pltpu.VMEM((pps, ATTN_W, PAGE), F32), pltpu.VMEM((pps, PAGE * HEADS, V_DIM), F32),
                        pltpu.SemaphoreType.DMA((2 * pps,)),
                        pltpu.VMEM((ATTN_W, pps // 2 * PAGE), BF16),
                        pltpu.VMEM((HEADS // 2, pps // 2 * PAGE, 2 * V_DIM), BF16),
                        pltpu.VMEM((n_rows, ATTN_W), BF16), pltpu.VMEM((n_rows, 1), F32),
                        pltpu.VMEM((n_rows, 1), F32), pltpu.VMEM((n_rows, V_DIM), F32)],
    )
    return pl.pallas_call(
        functools.partial(_ffn_stage_kernel, mix is not None, (pps, host_steps, seq0, n_pages)),
        grid_spec=grid_spec,
        out_shape=[out_shape, jax.ShapeDtypeStruct((n_host_seq * t, ATTN_W), F32)],
        compiler_params=_row_params(HOST_VMEM_LIMIT), name=name,
    )(page_table.reshape(-1), *args)


def _in_projection(x, g, w_ref, q_scale):
    h = _rms(x, g, RMS_EPS).astype(BF16)
    p = _dot(h, w_ref[...])
    q = p[:, :ATTN_W] * q_scale
    k = p[:, ATTN_W:2 * ATTN_W]
    v = p[:, 2 * ATTN_W:3 * ATTN_W]
    o = 3 * ATTN_W
    gate_b = p[:, o:o + CONV_W]
    u = p[:, o + CONV_W:o + 2 * CONV_W] * p[:, o + 2 * CONV_W:o + 3 * CONV_W]
    return q, k, v, gate_b, u


def _inproj_main_kernel(tiles_per_seq, x_ref, g_ref, w_ref, cw_ref, init_ref,
                        q_ref, kt_ref, v4_ref, kb_ref, vb_ref, c_ref, tail_ref, ubuf):
    tile = x_ref.shape[0]

    @pl.when(pl.program_id(0) % tiles_per_seq == 0)
    def _():
        ubuf[0:SUBLANES, :] = init_ref[...]

    q, k, v, gate_b, u = _in_projection(x_ref[...], g_ref[...], w_ref, QK_SCALE * LOG2E)
    q_ref[...] = q.astype(BF16)
    kt_ref[...] = k.T
    for h in range(HEADS):
        v4_ref[pl.ds(h, tile, stride=HEADS), :] = v[:, h * V_DIM:(h + 1) * V_DIM]
    kb_ref[...] = k.astype(BF16)
    vb_ref[...] = v.astype(BF16)
    ubuf[SUBLANES:SUBLANES + tile, :] = u
    u1 = ubuf[SUBLANES - 1:SUBLANES - 1 + tile, :]
    u2 = ubuf[SUBLANES - 2:SUBLANES - 2 + tile, :]
    y = cw_ref[0:1, :] * u2 + cw_ref[1:2, :] * u1 + cw_ref[2:3, :] * u
    c_ref[...] = (gate_b * y).astype(BF16)
    tail_ref[...] = u[tile - 2:, :]
    ubuf[0:SUBLANES, :] = u[tile - SUBLANES:, :]


def _inproj_extra_kernel(x_ref, g_ref, w_ref, cw_ref, p1_ref, p2_ref, m1_ref, m2_ref,
                         q_ref, kf_ref, vf_ref, c_ref, u_ref, ubuf):
    tile = x_ref.shape[0]
    q, k, v, gate_b, u = _in_projection(x_ref[...], g_ref[...], w_ref, QK_SCALE)
    q_ref[...] = q
    kf_ref[...] = k
    vf_ref[...] = v
    u_ref[...] = u
    ubuf[0:SUBLANES, :] = jnp.zeros((SUBLANES, CONV_W), F32)
    ubuf[SUBLANES:SUBLANES + tile, :] = u
    u1 = jnp.where(m1_ref[...] > 0.5, p1_ref[...], ubuf[SUBLANES - 1:SUBLANES - 1 + tile, :])
    u2 = jnp.where(m2_ref[...] > 0.5, p2_ref[...], ubuf[SUBLANES - 2:SUBLANES - 2 + tile, :])
    y = cw_ref[0:1, :] * u2 + cw_ref[1:2, :] * u1 + cw_ref[2:3, :] * u
    c_ref[...] = (gate_b * y).astype(BF16)


def _inproj_main_call(x1, g, w_in, conv_w, init, tile, rows_per_seq):
    rows = x1.shape[0]
    tiles_per_seq = rows_per_seq // tile
    n_seq = rows // rows_per_seq
    sd = jax.ShapeDtypeStruct
    return pl.pallas_call(
        functools.partial(_inproj_main_kernel, tiles_per_seq),
        grid=(rows // tile,),
        in_specs=[_row_spec(tile, D_MODEL), _const_spec((1, D_MODEL)), _const_spec((D_MODEL, IN_COLS)),
                  _const_spec((3, CONV_W)), _const_spec((SUBLANES, CONV_W))],
        out_specs=[_row_spec(tile, ATTN_W),
                   pl.BlockSpec((None, ATTN_W, tile), lambda i: (i // tiles_per_seq, 0, i % tiles_per_seq)),
                   _row_spec(tile * HEADS, V_DIM),
                   _row_spec(tile, ATTN_W), _row_spec(tile, ATTN_W), _row_spec(tile, CONV_W),
                   pl.BlockSpec((None, 2, CONV_W), lambda i: (i // tiles_per_seq, 0, 0))],
        out_shape=[sd((rows, ATTN_W), BF16), sd((n_seq, ATTN_W, rows_per_seq), F32),
                   sd((rows * HEADS, V_DIM), F32),
                   sd((rows, ATTN_W), BF16), sd((rows, ATTN_W), BF16), sd((rows, CONV_W), BF16),
                   sd((n_seq, 2, CONV_W), F32)],
        scratch_shapes=[pltpu.VMEM((SUBLANES + tile, CONV_W), F32)],
        compiler_params=_row_params(),
        name="inproj_main",
    )(x1, g, w_in, conv_w, init)


def _inproj_extra_call(x1, g, w_in, conv_w, p1, p2, m1, m2):
    rows = x1.shape[0]
    sd = jax.ShapeDtypeStruct
    full = lambda cols: pl.BlockSpec((rows, cols), lambda i: (0, 0))
    return pl.pallas_call(
        _inproj_extra_kernel,
        grid=(1,),
        in_specs=[full(D_MODEL), _const_spec((1, D_MODEL)), _const_spec((D_MODEL, IN_COLS)),
                  _const_spec((3, CONV_W))] + [full(CONV_W)] * 4,
        out_specs=[full(ATTN_W)] * 3 + [full(CONV_W)] * 2,
        out_shape=[sd((rows, ATTN_W), F32)] * 3 + [sd((rows, CONV_W), BF16), sd((rows, CONV_W), F32)],
        scratch_shapes=[pltpu.VMEM((SUBLANES + rows, CONV_W), F32)],
        compiler_params=_row_params(),
        name="inproj_extra",
    )(x1, g, w_in, conv_w, p1, p2, m1, m2)


def _stack_maps(q):
    lane = lax.broadcasted_iota(jnp.int32, q.shape, 1)
    zero = jnp.zeros_like(q)
    return jnp.concatenate([jnp.where(lane < HEAD_DIM, q, zero), jnp.where(lane >= HEAD_DIM, q, zero)], axis=0)


def _prompt_attn_kernel(q_ref, k_ref, v_ref, km_ref, vm_ref, lq1_ref, lk1_ref, lq2_ref, lk2_ref, g_ref,
                        o_ref, m_sc, acc_sc):
    tq = q_ref.shape[0]
    i = pl.program_id(1)
    qs = [_stack_maps(q_ref[:, h * V_DIM:(h + 1) * V_DIM]) for h in range(HEADS)]
    all_rows = [(slice(0, 2 * tq), None)]
    half_rows = [tuple(slice(m * tq + e * KV_TILE, m * tq + (e + 1) * KV_TILE) for m in range(2)) for e in range(2)]

    def update(h, rows, s, v_aug, first):
        m_cur = jnp.max(s, axis=1, keepdims=True)
        if first:
            m_new = jnp.broadcast_to(m_cur, (s.shape[0], LANES))
            p = jnp.exp2(s - m_cur)
            acc_sc[h, rows, :] = _dot(p.astype(BF16), v_aug)
        else:
            m_prev = m_sc[h, rows, :]
            m_new = jnp.maximum(m_prev, m_cur)
            alpha = jnp.exp2(m_prev - m_new)
            reps = s.shape[1] // LANES
            p = jnp.exp2(s - jnp.concatenate([m_new] * reps, axis=1))
            acc_sc[h, rows, :] = (jnp.concatenate([alpha, alpha], axis=1) * acc_sc[h, rows, :]
                                  + _dot(p.astype(BF16), v_aug))
        m_sc[h, rows, :] = m_new

    def block(k_blk, v_blk, row_groups, first):
        for h in range(HEADS):
            k = k_blk(h)
            v_aug = jnp.concatenate([v_blk(h), jnp.ones(k.shape, BF16)], axis=1)
            for rows, mask in row_groups:
                s = _dot_nt(qs[h][rows], k)
                if mask is not None:
                    s = jnp.where(mask, s, NEG_INF)
                update(h, rows, s, v_aug, first)

    def kv_block(j):
        start = pl.multiple_of(j * KV_TILE, KV_TILE)
        return (lambda h: k_ref[pl.ds(start, KV_TILE), h * V_DIM:(h + 1) * V_DIM],
                lambda h: v_ref[pl.ds(start, KV_TILE), h * V_DIM:(h + 1) * V_DIM])

    tiles_per_q = tq // KV_TILE
    n_full = i * tiles_per_q
    k_a, v_a = kv_block(n_full)
    row = lax.broadcasted_iota(jnp.int32, (KV_TILE, KV_TILE + PAGE), 0)
    col = lax.broadcasted_iota(jnp.int32, (KV_TILE, KV_TILE + PAGE), 1)
    meta_ok = (col >= KV_TILE) & (col < KV_TILE + N_META)
    first_masks = [((col <= row + e * KV_TILE) & (col < KV_TILE)) | meta_ok for e in range(2)]
    block(lambda h: jnp.concatenate([k_a(h), km_ref[:, h * V_DIM:(h + 1) * V_DIM]], axis=0),
          lambda h: jnp.concatenate([v_a(h), vm_ref[:, h * V_DIM:(h + 1) * V_DIM]], axis=0),
          [(rows, first_masks[e]) for e in range(2) for rows in half_rows[e]], True)
    causal = (lax.broadcasted_iota(jnp.int32, (KV_TILE, KV_TILE), 1)
              <= lax.broadcasted_iota(jnp.int32, (KV_TILE, KV_TILE), 0))
    block(*kv_block(n_full + 1), [(rows, causal) for rows in half_rows[1]], False)

    rem = n_full % KV_UNROLL
    width = tiles_per_q
    while width < KV_UNROLL:
        def leftover(width=width):
            first_blk = (n_full // KV_UNROLL) * KV_UNROLL + (rem & (width - 1))
            for d in range(width):
                block(*kv_block(first_blk + d), all_rows, False)
        pl.when(rem & width != 0)(leftover)
        width *= 2

    def full_group(jj, carry):
        for d in range(KV_UNROLL):
            block(*kv_block(KV_UNROLL * jj + d), all_rows, False)
        return carry

    lax.fori_loop(0, n_full // KV_UNROLL, full_group, 0)

    lam = _lambda(lq1_ref, lk1_ref, lq2_ref, lk2_ref)
    for h in range(HEADS):
        acc = acc_sc[h]
        o = acc[:, :V_DIM] / acc[:, V_DIM:]
        o_ref[:, h * V_DIM:(h + 1) * V_DIM] = _head_out(o[:tq], o[tq:], lam, g_ref[...]).astype(o_ref.dtype)


def _prompt_attn_call(q, k, v, km, vm, lq1, lk1, lq2, lk2, subln_g):
    b, s, _ = q.shape
    small = lambda shape: pl.BlockSpec(shape, lambda b_, i: (0,) * len(shape))
    resident = pl.BlockSpec((None, s, ATTN_W), lambda b_, i: (b_, 0, 0), pipeline_mode=pl.Buffered(1))
    return pl.pallas_call(
        _prompt_attn_kernel,
        grid=(b, s // Q_TILE),
        in_specs=[pl.BlockSpec((None, Q_TILE, ATTN_W), lambda b_, i: (b_, i, 0)), resident, resident,
                  small((PAGE, ATTN_W)), small((PAGE, ATTN_W)),
                  small((1, HEAD_DIM)), small((1, HEAD_DIM)), small((1, HEAD_DIM)), small((1, HEAD_DIM)),
                  small((1, V_DIM))],
        out_specs=pl.BlockSpec((None, Q_TILE, ATTN_W), lambda b_, i: (b_, i, 0)),
        out_shape=jax.ShapeDtypeStruct((b, s, ATTN_W), BF16),
        scratch_shapes=[pltpu.VMEM((HEADS, 2 * Q_TILE, LANES), F32),
                        pltpu.VMEM((HEADS, 2 * Q_TILE, 2 * V_DIM), F32)],
        compiler_params=pltpu.CompilerParams(dimension_semantics=("arbitrary",) * 2,
                                             vmem_limit_bytes=VMEM_LIMIT),
        name="prompt_attn",
    )(q, k, v, km, vm, lq1, lk1, lq2, lk2, subln_g)


def _meta_attn_kernel(q_ref, k_ref, v_ref, lq1_ref, lk1_ref, lq2_ref, lk2_ref, g_ref, o_ref):
    t = q_ref.shape[0]
    qs = _stack_maps(q_ref[...].astype(BF16))
    pad = jnp.zeros((PAGE - t, V_DIM), F32)
    k = jnp.concatenate([k_ref[...], pad], axis=0).astype(BF16)
    v = jnp.concatenate([v_ref[...], pad], axis=0).astype(BF16)
    s = _dot_nt(qs, k)
    row = lax.broadcasted_iota(jnp.int32, s.shape, 0)
    col = lax.broadcasted_iota(jnp.int32, s.shape, 1)
    qrow = jnp.where(row >= t, row - t, row)
    s = jnp.where(col <= qrow, s, NEG_INF)
    p = jnp.exp(s - jnp.max(s, axis=-1, keepdims=True))
    o = _dot(p.astype(BF16), v) / jnp.sum(p, axis=-1, keepdims=True)
    lam = _lambda(lq1_ref, lk1_ref, lq2_ref, lk2_ref)
    o_ref[...] = _head_out(o[:t], o[t:], lam, g_ref[...])


def _meta_attn_call(q, k, v, lq1, lk1, lq2, lk2, subln_g):
    small = lambda shape: pl.BlockSpec(shape, lambda h: (0,) * len(shape))
    head = pl.BlockSpec((N_META, V_DIM), lambda h: (0, h))
    return pl.pallas_call(
        _meta_attn_kernel,
        grid=(HEADS,),
        in_specs=[head, head, head, small((1, HEAD_DIM)), small((1, HEAD_DIM)), small((1, HEAD_DIM)),
                  small((1, HEAD_DIM)), small((1, V_DIM))],
        out_specs=head,
        out_shape=jax.ShapeDtypeStruct((N_META, ATTN_W), F32),
        compiler_params=pltpu.CompilerParams(dimension_semantics=("arbitrary",)),
        name="meta_attn",
    )(q, k, v, lq1, lk1, lq2, lk2, subln_g)


def kernel(x_prompt, x_sample, cache_k, cache_v, state_conv, page_table, meta_tokens, ffn1_pre_g, ffn1_w_gate, ffn1_w_up, ffn1_w_down, ffn1_post_g, mix_pre_g, w_in, lambda_q1, lambda_k1, lambda_q2, lambda_k2, subln_g, conv_w, w_out, mix_post_g, ffn2_pre_g, ffn2_w_gate, ffn2_w_up, ffn2_w_down, ffn2_post_g):
    batch, seq, _ = x_prompt.shape
    n_seq, t_new, _ = x_sample.shape
    n_pool = cache_k.shape[1]
    l = 0

    bf = lambda w: w[l].astype(BF16)
    f1 = (ffn1_pre_g, bf(ffn1_w_gate), bf(ffn1_w_up), bf(ffn1_w_down), ffn1_post_g)
    f2 = (ffn2_pre_g, bf(ffn2_w_gate), bf(ffn2_w_up), bf(ffn2_w_down), ffn2_post_g)
    w_in_b, w_out_b = bf(w_in), bf(w_out)
    lam_args = (lambda_q1, lambda_k1, lambda_q2, lambda_k2, subln_g)
    ckt = jnp.transpose(cache_k[l], (0, 2, 3, 4, 1)).reshape(n_pool, ATTN_W, PAGE)
    cv = cache_v[l].reshape(n_pool, PAGE * HEADS, V_DIM)

    n_s = n_seq * t_new
    n_x = N_META + n_s
    xe = jnp.concatenate([meta_tokens, x_sample.reshape(n_s, D_MODEL)], axis=0)
    st = state_conv[l]
    zpad = lambda a, n: jnp.concatenate([a, jnp.zeros((n_seq, n, CONV_W), F32)], axis=1).reshape(n_s, CONV_W)
    zmeta = jnp.zeros((N_META, CONV_W), F32)
    p1 = jnp.concatenate([zmeta, zpad(st[:, 1:2], t_new - 1)], axis=0)
    p2 = jnp.concatenate([zmeta, zpad(st, t_new - 2)], axis=0)
    r = jnp.arange(n_x)[:, None]
    in_sample = r >= N_META
    pos = jnp.where(in_sample, (r - N_META) % t_new, r)
    m1 = jnp.broadcast_to((pos < 1).astype(F32), (n_x, CONV_W))
    m2 = jnp.broadcast_to((pos < 2).astype(F32), (n_x, CONV_W))

    xe1 = _ffn_stage_call(xe, f1, n_x, name="ffn1_extra")
    qe, ke, ve, ce, ue = _inproj_extra_call(xe1, mix_pre_g, w_in_b, conv_w[l], p1, p2, m1, m2)
    ae_meta = _meta_attn_call(qe[:N_META], ke[:N_META], ve[:N_META], *lam_args)

    rows = batch * seq
    half = n_seq // 2
    host = lambda seq0: (page_table, seq0, half, qe, ke, ve, N_META, t_new, ckt, cv, lam_args)
    xp = x_prompt.reshape(rows, D_MODEL)
    xp1, ae_s0 = _ffn_stage_call(xp, f1, HOST_ROW_TILE, host=host(0), name="ffn1")
    init = jnp.concatenate([jnp.zeros((SUBLANES - 2, CONV_W), F32), ue[N_META - 2:N_META]], axis=0)
    qp, kpt, vp4, kpb, vpb, cp, tail = _inproj_main_call(xp1, mix_pre_g, w_in_b, conv_w[l], init,
                                                        tile=ROW_TILE, rows_per_seq=seq)
    meta_pad = lambda a: jnp.concatenate([a[:N_META], jnp.zeros((PAGE - N_META, ATTN_W), F32)], axis=0).astype(BF16)
    shp = (batch, seq, ATTN_W)
    ap = _prompt_attn_call(qp.reshape(shp), kpb.reshape(shp), vpb.reshape(shp), meta_pad(ke), meta_pad(ve),
                           *lam_args)
    yp, ae_s1 = _ffn_stage_call(xp1, f2, HOST_ROW_TILE, mix=(ap.reshape(rows, ATTN_W), cp, w_out_b, mix_post_g),
                                host=host(half), name="mixout_ffn2")
    ae = jnp.concatenate([ae_meta, ae_s0, ae_s1], axis=0)
    ye = _ffn_stage_call(xe1, f2, n_x, mix=(ae, ce, w_out_b, mix_post_g), name="mixout_ffn2_extra")

    y_prompt = yp.reshape(batch, seq, D_MODEL)
    y_sample = ye[N_META:].reshape(n_seq, t_new, D_MODEL)
    kt_meta = jnp.broadcast_to(ke[:N_META].T[None], (batch, ATTN_W, N_META))
    kt_all = jnp.concatenate([kt_meta, kpt], axis=2).reshape(batch, HEADS, 2, HEAD_DIM, N_META + seq)
    k_prompt_new = jnp.transpose(kt_all, (0, 4, 1, 2, 3))[None]
    v_meta = jnp.broadcast_to(ve[:N_META].reshape(1, N_META, HEADS, V_DIM), (batch, N_META, HEADS, V_DIM))
    v_prompt_new = jnp.concatenate([v_meta, vp4.reshape(batch, seq, HEADS, V_DIM)], axis=1)[None]
    conv_prompt_new = tail[None]
    k_sample_new = ke[N_META:].reshape(1, n_seq, t_new, HEADS, 2, HEAD_DIM)
    v_sample_new = ve[N_META:].reshape(1, n_seq, t_new, HEADS, V_DIM)
    conv_sample_new = ue[N_META:].reshape(n_seq, t_new, CONV_W)[None, :, t_new - 2:, :]
    return (y_prompt, y_sample, k_prompt_new, v_prompt_new, conv_prompt_new,
            k_sample_new, v_sample_new, conv_sample_new)
```

```python
import functools
import math

import jax
import jax.numpy as jnp
from jax import lax
from jax.experimental import pallas as pl
from jax.experimental.pallas import tpu as pltpu

F32 = jnp.float32
BF16 = jnp.bfloat16

D_MODEL = 1024
D_FF = 2816
N_META = 16
HEADS = 4
HEAD_DIM = 64
V_DIM = 2 * HEAD_DIM
ATTN_W = HEADS * V_DIM
CONV_W = D_MODEL - ATTN_W
IN_COLS = 3 * ATTN_W + 3 * CONV_W
PAGE = 128
RMS_EPS = 1e-6
SUBLN_EPS = 1e-5
NEG_INF = -1e30
LAMBDA_INIT = 0.8 - 0.6 * math.exp(-0.3 * 0)
QK_SCALE = HEAD_DIM ** -0.5
LOG2E = math.log2(math.e)

SUBLANES = 8
LANES = 128
VMEM_LIMIT = 56 * 1024 * 1024
HOST_VMEM_LIMIT = 62 * 1024 * 1024

ROW_TILE = 512
HOST_ROW_TILE = 512
MXU_TILE = 256
FF_CHUNKS = ((0, 6 * MXU_TILE), (6 * MXU_TILE, D_FF))
KV_TILE = 256
Q_TILE = 2 * KV_TILE
KV_UNROLL = 4


def _rms(x, g, eps):
    ms = jnp.mean(x * x, axis=-1, keepdims=True)
    return x * lax.rsqrt(ms + eps) * g


def _dot(a, b):
    return jnp.dot(a, b, preferred_element_type=F32)


def _dot_nt(a, b):
    return lax.dot_general(a, b, (((1,), (1,)), ((), ())), preferred_element_type=F32)


def _ffn_half_step(x, pre_g, wg_ref, wu_ref, wd_ref, post_g, side_work=()):
    side_work = list(side_work)
    run_side = lambda: side_work.pop(0)() if side_work else None
    h = _rms(x, pre_g, RMS_EPS).astype(BF16)
    acc = None
    for lo, hi in FF_CHUNKS:
        sl = slice(lo, hi)
        g = _dot(h, wg_ref[:, sl])
        run_side()
        u = _dot(h, wu_ref[:, sl])
        run_side()
        a = (g * jax.nn.sigmoid(g) * u).astype(BF16)
        d = _dot(a, wd_ref[sl, :])
        run_side()
        acc = d if acc is None else acc + d
    assert not side_work
    return x + 0.5 * _rms(acc, post_g, RMS_EPS)


def _lambda(lq1_ref, lk1_ref, lq2_ref, lk2_ref):
    a = jnp.exp(jnp.sum(lq1_ref[...] * lk1_ref[...], axis=-1, keepdims=True))
    b = jnp.exp(jnp.sum(lq2_ref[...] * lk2_ref[...], axis=-1, keepdims=True))
    return a - b + LAMBDA_INIT


def _head_out(o1, o2, lam, g):
    o = o1 - lam * o2
    return _rms(o, g, SUBLN_EPS) * (1.0 - LAMBDA_INIT)


class _SampleAttn:
    N_REFS = 8

    def __init__(self, in_refs, k_refs, v_refs, o_ref, qbd_sc, m_sc, l_sc, acc_sc):
        self.q_ref, self.kn_ref, self.vn_ref = in_refs[:3]
        self.lam_refs, self.g_ref = in_refs[3:7], in_refs[7]
        self.k_refs, self.v_refs, self.o_ref = k_refs, v_refs, o_ref
        self.qbd_sc, self.m_sc, self.l_sc, self.acc_sc = qbd_sc, m_sc, l_sc, acc_sc
        self.t = self.q_ref.shape[0]

    def _softmax(self, s):
        m_prev = self.m_sc[...]
        m_new = jnp.maximum(m_prev, jnp.max(s, axis=1, keepdims=True))
        alpha = jnp.exp(m_prev - m_new)
        p = jnp.exp(s - m_new)
        self.l_sc[...] = alpha * self.l_sc[...] + jnp.sum(p, axis=1, keepdims=True)
        self.m_sc[...] = m_new
        return alpha, p.astype(BF16)

    def _values(self, g, alpha, pb, v_pair):
        t, acc_sc = self.t, self.acc_sc
        pv = _dot(pb[g * 4 * t:(g + 1) * 4 * t], v_pair)
        for e in range(2):
            rs = slice((2 * g + e) * 2 * t, (2 * g + e + 1) * 2 * t)
            acc_sc[rs, :] = (alpha[rs] * acc_sc[rs, :]
                             + pv[e * 2 * t:(e + 1) * 2 * t, e * V_DIM:(e + 1) * V_DIM])

    def start(self, j):
        t = self.t
        n_rows = HEADS * 2 * t

        @pl.when(j == 0)
        def _():
            qt = jnp.concatenate([self.q_ref[...]] * (n_rows // t), axis=0)
            row = lax.broadcasted_iota(jnp.int32, qt.shape, 0)
            col = lax.broadcasted_iota(jnp.int32, qt.shape, 1)
            self.qbd_sc[...] = jnp.where(col // HEAD_DIM == row // t, qt, 0.0).astype(BF16)
            self.m_sc[...] = jnp.full(self.m_sc.shape, NEG_INF, F32)
            self.l_sc[...] = jnp.zeros(self.l_sc.shape, F32)
            self.acc_sc[...] = jnp.zeros(self.acc_sc.shape, F32)

    def page_work(self, slots, kb_sc, vb_sc):
        st = {}

        def cast():
            for n, r in enumerate(slots):
                cols = slice(n * PAGE, (n + 1) * PAGE)
                kb_sc[:, cols] = self.k_refs[r][...].astype(BF16)
                for h in range(HEADS):
                    vb_sc[h // 2, cols, (h % 2) * V_DIM:(h % 2 + 1) * V_DIM] = (
                        self.v_refs[r][pl.ds(h, PAGE, stride=HEADS), :].astype(BF16))

        def scores():
            st["s"] = _dot(self.qbd_sc[...], kb_sc[...])

        def values():
            alpha, pb = self._softmax(st["s"])
            for g in range(HEADS // 2):
                self._values(g, alpha, pb, vb_sc[g])

        return cast, scores, values

    def finish(self, j, n_steps):
        t = self.t

        @pl.when(j == n_steps - 1)
        def _():
            pad = jnp.zeros((PAGE - t, ATTN_W), F32)
            kn = jnp.concatenate([self.kn_ref[...], pad], axis=0).astype(BF16)
            vn = jnp.concatenate([self.vn_ref[...], pad], axis=0).astype(BF16)
            s = _dot_nt(self.qbd_sc[...], kn)
            row = lax.broadcasted_iota(jnp.int32, s.shape, 0)
            key = lax.broadcasted_iota(jnp.int32, s.shape, 1)
            s = jnp.where((key < t) & (key <= row % t), s, NEG_INF)
            alpha, pb = self._softmax(s)
            for g in range(HEADS // 2):
                self._values(g, alpha, pb, vn[:, g * 2 * V_DIM:(g + 1) * 2 * V_DIM])
            o = self.acc_sc[...] / self.l_sc[...]
            lam = _lambda(*self.lam_refs)
            for h in range(HEADS):
                r0 = h * 2 * t
                self.o_ref[:, h * V_DIM:(h + 1) * V_DIM] = _head_out(
                    o[r0:r0 + t], o[r0 + t:r0 + 2 * t], lam, self.g_ref[...])


def _ffn_stage_kernel(has_mix, host, *refs):
    refs = list(refs)
    take = lambda n: [refs.pop(0) for _ in range(n)]
    if host:
        (pt_ref,) = take(1)
    (x_ref,) = take(1)
    if has_mix:
        a_ref, c_ref, wo_ref, mpost_ref = take(4)
    pre_ref, wg_ref, wu_ref, wd_ref, post_ref = take(5)
    if host:
        sample_refs = take(_SampleAttn.N_REFS)
        kt_hbm, v_hbm = take(2)
    (o_ref,) = take(1)
    side_work = ()
    if host:
        pps, host_steps, seq0, n_pages = host
        ao_ref, kbuf, vbuf, sems, kb_sc, vb_sc = take(6)
        i = pl.program_id(0)
        last = pl.num_programs(0) - 1
        j = i % host_steps
        halves = (range(0, pps // 2), range(pps // 2, pps))

        def copies(step, slots):
            first_page = (seq0 + step // host_steps) * n_pages + (step % host_steps) * pps
            out = []
            for r in slots:
                page = pt_ref[first_page + r]
                out.append(pltpu.make_async_copy(kt_hbm.at[page], kbuf.at[r], sems.at[r]))
                out.append(pltpu.make_async_copy(v_hbm.at[page], vbuf.at[r], sems.at[pps + r]))
            return out

        start = lambda cs: [c.start() for c in cs]
        wait = lambda cs: [c.wait() for c in cs]
        sample = _SampleAttn(sample_refs, [kbuf.at[r] for r in range(pps)], [vbuf.at[r] for r in range(pps)],
                             ao_ref, *refs)

        @pl.when(i == 0)
        def _():
            start(copies(i, halves[0]))

        sample.start(j)
        first_half, second_half = copies(i, halves[0]), copies(i, halves[1])
        next_first_half = copies(jnp.minimum(i + 1, last), halves[0])
        start(second_half)
        cast_a, scores_a, values_a = sample.page_work(halves[0], kb_sc, vb_sc)
        cast_b, scores_b, values_b = sample.page_work(halves[1], kb_sc, vb_sc)
        wait(first_half)
        cast_a()
        start(next_first_half)
        side_work = [scores_a, values_a, lambda: (wait(second_half), cast_b()), scores_b, values_b]
    x = x_ref[...]
    if has_mix:
        m = _dot(a_ref[...].astype(BF16), wo_ref[:ATTN_W, :]) + _dot(c_ref[...], wo_ref[ATTN_W:, :])
        x = x + _rms(m, mpost_ref[...], RMS_EPS)
    o_ref[...] = _ffn_half_step(x, pre_ref[...], wg_ref, wu_ref, wd_ref, post_ref[...], side_work)
    if host:
        sample.finish(j, host_steps)

        @pl.when(i == last)
        def _():
            wait(next_first_half)


def _const_spec(shape):
    return pl.BlockSpec(shape, lambda *_: (0,) * len(shape), pipeline_mode=pl.Buffered(1))


def _row_spec(tile, cols):
    return pl.BlockSpec((tile, cols), lambda i, *_: (i, 0))


def _ffn_weight_specs():
    return [_const_spec((1, D_MODEL)), _const_spec((D_MODEL, D_FF)), _const_spec((D_MODEL, D_FF)),
            _const_spec((D_FF, D_MODEL)), _const_spec((1, D_MODEL))]


def _row_params(vmem_limit=VMEM_LIMIT):
    return pltpu.CompilerParams(dimension_semantics=("arbitrary",), vmem_limit_bytes=vmem_limit)


def _ffn_stage_call(x, ffn, tile, mix=None, host=None, name="ffn"):
    rows = x.shape[0]
    steps = rows // tile
    args = [x]
    in_specs = [_row_spec(tile, D_MODEL)]
    if mix is not None:
        args += list(mix)
        in_specs += [_row_spec(tile, ATTN_W), _row_spec(tile, CONV_W),
                     _const_spec((D_MODEL, D_MODEL)), _const_spec((1, D_MODEL))]
    args += list(ffn)
    in_specs += _ffn_weight_specs()
    out_specs = _row_spec(tile, D_MODEL)
    out_shape = jax.ShapeDtypeStruct((rows, D_MODEL), F32)
    if host is None:
        return pl.pallas_call(
            functools.partial(_ffn_stage_kernel, mix is not None, None),
            grid=(steps,), in_specs=in_specs, out_specs=out_specs, out_shape=out_shape,
            compiler_params=_row_params(), name=name,
        )(*args)

    page_table, seq0, n_host_seq, q, k_new, v_new, row0, t, cache_kt, cache_v, lam_args = host
    n_pages = page_table.shape[1]
    assert steps % n_host_seq == 0 and row0 % t == 0
    host_steps = steps // n_host_seq
    pps = n_pages // host_steps
    assert pps * host_steps == n_pages
    blk0 = row0 // t + seq0
    n_rows = HEADS * 2 * t
    seq_spec = pl.BlockSpec((t, ATTN_W), lambda i, pt: (blk0 + i // host_steps, 0))
    small = lambda shape: pl.BlockSpec(shape, lambda i, pt: (0,) * len(shape))
    in_hbm = pl.BlockSpec(memory_space=pl.ANY)

    args += [q, k_new, v_new, *lam_args, cache_kt, cache_v]
    in_specs += [seq_spec] * 3 + [small((1, HEAD_DIM))] * 4 + [small((1, V_DIM))] + [in_hbm] * 2
    grid_spec = pltpu.PrefetchScalarGridSpec(
        num_scalar_prefetch=1,
        grid=(steps,),
        in_specs=in_specs,
        out_specs=[out_specs, pl.BlockSpec((t, ATTN_W), lambda i, pt: (i // host_steps, 0))],
        scratch_shapes=[pltpu.VMEM((pps, ATTN_W, PAGE), F32), pltpu.VMEM((pps, PAGE * HEADS, V_DIM), F32),
                        pltpu.SemaphoreType.DMA((2 * pps,)),
                        pltpu.VMEM((ATTN_W, pps // 2 * PAGE), BF16),
                        pltpu.VMEM((HEADS // 2, pps // 2 * PAGE, 2 * V_DIM), BF16),
                        pltpu.VMEM((n_rows, ATTN_W), BF16), pltpu.VMEM((n_rows, 1), F32),
                        pltpu.VMEM((n_rows, 1), F32), pltpu.VMEM((n_rows, V_DIM), F32)],
    )
    return pl.pallas_call(
        functools.partial(_ffn_stage_kernel, mix is not None, (pps, host_steps, seq0, n_pages)),
        grid_spec=grid_spec,
        out_shape=[out_shape, jax.ShapeDtypeStruct((n_host_seq * t, ATTN_W), F32)],
        compiler_params=_row_params(HOST_VMEM_LIMIT), name=name,
    )(page_table.reshape(-1), *args)


def _in_projection(x, g, w_ref, q_scale):
    h = _rms(x, g, RMS_EPS).astype(BF16)
    p = _dot(h, w_ref[...])
    q = p[:, :ATTN_W] * q_scale
    k = p[:, ATTN_W:2 * ATTN_W]
    v = p[:, 2 * ATTN_W:3 * ATTN_W]
    o = 3 * ATTN_W
    gate_b = p[:, o:o + CONV_W]
    u = p[:, o + CONV_W:o + 2 * CONV_W] * p[:, o + 2 * CONV_W:o + 3 * CONV_W]
    return q, k, v, gate_b, u


def _inproj_main_kernel(tiles_per_seq, x_ref, g_ref, w_ref, cw_ref, init_ref,
                        q_ref, kt_ref, v4_ref, kb_ref, vb_ref, c_ref, tail_ref, ubuf):
    tile = x_ref.shape[0]

    @pl.when(pl.program_id(0) % tiles_per_seq == 0)
    def _():
        ubuf[0:SUBLANES, :] = init_ref[...]

    q, k, v, gate_b, u = _in_projection(x_ref[...], g_ref[...], w_ref, QK_SCALE * LOG2E)
    q_ref[...] = q.astype(BF16)
    kt_ref[...] = k.T
    for h in range(HEADS):
        v4_ref[pl.ds(h, tile, stride=HEADS), :] = v[:, h * V_DIM:(h + 1) * V_DIM]
    kb_ref[...] = k.astype(BF16)
    vb_ref[...] = v.astype(BF16)
    ubuf[SUBLANES:SUBLANES + tile, :] = u
    u1 = ubuf[SUBLANES - 1:SUBLANES - 1 + tile, :]
    u2 = ubuf[SUBLANES - 2:SUBLANES - 2 + tile, :]
    y = cw_ref[0:1, :] * u2 + cw_ref[1:2, :] * u1 + cw_ref[2:3, :] * u
    c_ref[...] = (gate_b * y).astype(BF16)
    tail_ref[...] = u[tile - 2:, :]
    ubuf[0:SUBLANES, :] = u[tile - SUBLANES:, :]


def _inproj_extra_kernel(x_ref, g_ref, w_ref, cw_ref, p1_ref, p2_ref, m1_ref, m2_ref,
                         q_ref, kf_ref, vf_ref, c_ref, u_ref, ubuf):
    tile = x_ref.shape[0]
    q, k, v, gate_b, u = _in_projection(x_ref[...], g_ref[...], w_ref, QK_SCALE)
    q_ref[...] = q
    kf_ref[...] = k
    vf_ref[...] = v
    u_ref[...] = u
    ubuf[0:SUBLANES, :] = jnp.zeros((SUBLANES, CONV_W), F32)
    ubuf[SUBLANES:SUBLANES + tile, :] = u
    u1 = jnp.where(m1_ref[...] > 0.5, p1_ref[...], ubuf[SUBLANES - 1:SUBLANES - 1 + tile, :])
    u2 = jnp.where(m2_ref[...] > 0.5, p2_ref[...], ubuf[SUBLANES - 2:SUBLANES - 2 + tile, :])
    y = cw_ref[0:1, :] * u2 + cw_ref[1:2, :] * u1 + cw_ref[2:3, :] * u
    c_ref[...] = (gate_b * y).astype(BF16)


def _inproj_main_call(x1, g, w_in, conv_w, init, tile, rows_per_seq):
    rows = x1.shape[0]
    tiles_per_seq = rows_per_seq // tile
    n_seq = rows // rows_per_seq
    sd = jax.ShapeDtypeStruct
    return pl.pallas_call(
        functools.partial(_inproj_main_kernel, tiles_per_seq),
        grid=(rows // tile,),
        in_specs=[_row_spec(tile, D_MODEL), _const_spec((1, D_MODEL)), _const_spec((D_MODEL, IN_COLS)),
                  _const_spec((3, CONV_W)), _const_spec((SUBLANES, CONV_W))],
        out_specs=[_row_spec(tile, ATTN_W),
                   pl.BlockSpec((None, ATTN_W, tile), lambda i: (i // tiles_per_seq, 0, i % tiles_per_seq)),
                   _row_spec(tile * HEADS, V_DIM),
                   _row_spec(tile, ATTN_W), _row_spec(tile, ATTN_W), _row_spec(tile, CONV_W),
                   pl.BlockSpec((None, 2, CONV_W), lambda i: (i // tiles_per_seq, 0, 0))],
        out_shape=[sd((rows, ATTN_W), BF16), sd((n_seq, ATTN_W, rows_per_seq), F32),
                   sd((rows * HEADS, V_DIM), F32),
                   sd((rows, ATTN_W), BF16), sd((rows, ATTN_W), BF16), sd((rows, CONV_W), BF16),
                   sd((n_seq, 2, CONV_W), F32)],
        scratch_shapes=[pltpu.VMEM((SUBLANES + tile, CONV_W), F32)],
        compiler_params=_row_params(),
        name="inproj_main",
    )(x1, g, w_in, conv_w, init)


def _inproj_extra_call(x1, g, w_in, conv_w, p1, p2, m1, m2):
    rows = x1.shape[0]
    sd = jax.ShapeDtypeStruct
    full = lambda cols: pl.BlockSpec((rows, cols), lambda i: (0, 0))
    return pl.pallas_call(
        _inproj_extra_kernel,
        grid=(1,),
        in_specs=[full(D_MODEL), _const_spec((1, D_MODEL)), _const_spec((D_MODEL, IN_COLS)),
                  _const_spec((3, CONV_W))] + [full(CONV_W)] * 4,
        out_specs=[full(ATTN_W)] * 3 + [full(CONV_W)] * 2,
        out_shape=[sd((rows, ATTN_W), F32)] * 3 + [sd((rows, CONV_W), BF16), sd((rows, CONV_W), F32)],
        scratch_shapes=[pltpu.VMEM((SUBLANES + rows, CONV_W), F32)],
        compiler_params=_row_params(),
        name="inproj_extra",
    )(x1, g, w_in, conv_w, p1, p2, m1, m2)


def _stack_maps(q):
    lane = lax.broadcasted_iota(jnp.int32, q.shape, 1)
    zero = jnp.zeros_like(q)
    return jnp.concatenate([jnp.where(lane < HEAD_DIM, q, zero), jnp.where(lane >= HEAD_DIM, q, zero)], axis=0)


def _prompt_attn_kernel(q_ref, k_ref, v_ref, km_ref, vm_ref, lq1_ref, lk1_ref, lq2_ref, lk2_ref, g_ref,
                        o_ref, m_sc, acc_sc):
    tq = q_ref.shape[0]
    i = pl.program_id(1)
    qs = [_stack_maps(q_ref[:, h * V_DIM:(h + 1) * V_DIM]) for h in range(HEADS)]
    all_rows = [(slice(0, 2 * tq), None)]
    half_rows = [tuple(slice(m * tq + e * KV_TILE, m * tq + (e + 1) * KV_TILE) for m in range(2)) for e in range(2)]

    def update(h, rows, s, v_aug, first):
        m_cur = jnp.max(s, axis=1, keepdims=True)
        if first:
            m_new = jnp.broadcast_to(m_cur, (s.shape[0], LANES))
            p = jnp.exp2(s - m_cur)
            acc_sc[h, rows, :] = _dot(p.astype(BF16), v_aug)
        else:
            m_prev = m_sc[h, rows, :]
            m_new = jnp.maximum(m_prev, m_cur)
            alpha = jnp.exp2(m_prev - m_new)
            reps = s.shape[1] // LANES
            p = jnp.exp2(s - jnp.concatenate([m_new] * reps, axis=1))
            acc_sc[h, rows, :] = (jnp.concatenate([alpha, alpha], axis=1) * acc_sc[h, rows, :]
                                  + _dot(p.astype(BF16), v_aug))
        m_sc[h, rows, :] = m_new

    def block(k_blk, v_blk, row_groups, first):
        for h in range(HEADS):
            k = k_blk(h)
            v_aug = jnp.concatenate([v_blk(h), jnp.ones(k.shape, BF16)], axis=1)
            for rows, mask in row_groups:
                s = _dot_nt(qs[h][rows], k)
                if mask is not None:
                    s = jnp.where(mask, s, NEG_INF)
                update(h, rows, s, v_aug, first)

    def kv_block(j):
        start = pl.multiple_of(j * KV_TILE, KV_TILE)
        return (lambda h: k_ref[pl.ds(start, KV_TILE), h * V_DIM:(h + 1) * V_DIM],
                lambda h: v_ref[pl.ds(start, KV_TILE), h * V_DIM:(h + 1) * V_DIM])

    tiles_per_q = tq // KV_TILE
    n_full = i * tiles_per_q
    k_a, v_a = kv_block(n_full)
    row = lax.broadcasted_iota(jnp.int32, (KV_TILE, KV_TILE + PAGE), 0)
    col = lax.broadcasted_iota(jnp.int32, (KV_TILE, KV_TILE + PAGE), 1)
    meta_ok = (col >= KV_TILE) & (col < KV_TILE + N_META)
    first_masks = [((col <= row + e * KV_TILE) & (col < KV_TILE)) | meta_ok for e in range(2)]
    block(lambda h: jnp.concatenate([k_a(h), km_ref[:, h * V_DIM:(h + 1) * V_DIM]], axis=0),
          lambda h: jnp.concatenate([v_a(h), vm_ref[:, h * V_DIM:(h + 1) * V_DIM]], axis=0),
          [(rows, first_masks[e]) for e in range(2) for rows in half_rows[e]], True)
    causal = (lax.broadcasted_iota(jnp.int32, (KV_TILE, KV_TILE), 1)
              <= lax.broadcasted_iota(jnp.int32, (KV_TILE, KV_TILE), 0))
    block(*kv_block(n_full + 1), [(rows, causal) for rows in half_rows[1]], False)

    rem = n_full % KV_UNROLL
    width = tiles_per_q
    while width < KV_UNROLL:
        def leftover(width=width):
            first_blk = (n_full // KV_UNROLL) * KV_UNROLL + (rem & (width - 1))
            for d in range(width):
                block(*kv_block(first_blk + d), all_rows, False)
        pl.when(rem & width != 0)(leftover)
        width *= 2

    def full_group(jj, carry):
        for d in range(KV_UNROLL):
            block(*kv_block(KV_UNROLL * jj + d), all_rows, False)
        return carry

    lax.fori_loop(0, n_full // KV_UNROLL, full_group, 0)

    lam = _lambda(lq1_ref, lk1_ref, lq2_ref, lk2_ref)
    for h in range(HEADS):
        acc = acc_sc[h]
        o = acc[:, :V_DIM] / acc[:, V_DIM:]
        o_ref[:, h * V_DIM:(h + 1) * V_DIM] = _head_out(o[:tq], o[tq:], lam, g_ref[...]).astype(o_ref.dtype)


def _prompt_attn_call(q, k, v, km, vm, lq1, lk1, lq2, lk2, subln_g):
    b, s, _ = q.shape
    small = lambda shape: pl.BlockSpec(shape, lambda b_, i: (0,) * len(shape))
    resident = pl.BlockSpec((None, s, ATTN_W), lambda b_, i: (b_, 0, 0))
    return pl.pallas_call(
        _prompt_attn_kernel,
        grid=(b, s // Q_TILE),
        in_specs=[pl.BlockSpec((None, Q_TILE, ATTN_W), lambda b_, i: (b_, i, 0)), resident, resident,
                  small((PAGE, ATTN_W)), small((PAGE, ATTN_W)),
                  small((1, HEAD_DIM)), small((1, HEAD_DIM)), small((1, HEAD_DIM)), small((1, HEAD_DIM)),
                  small((1, V_DIM))],
        out_specs=pl.BlockSpec((None, Q_TILE, ATTN_W), lambda b_, i: (b_, i, 0)),
        out_shape=jax.ShapeDtypeStruct((b, s, ATTN_W), BF16),
        scratch_shapes=[pltpu.VMEM((HEADS, 2 * Q_TILE, LANES), F32),
                        pltpu.VMEM((HEADS, 2 * Q_TILE, 2 * V_DIM), F32)],
        compiler_params=pltpu.CompilerParams(dimension_semantics=("arbitrary",) * 2,
                                             vmem_limit_bytes=VMEM_LIMIT),
        name="prompt_attn",
    )(q, k, v, km, vm, lq1, lk1, lq2, lk2, subln_g)


def _meta_attn_kernel(q_ref, k_ref, v_ref, lq1_ref, lk1_ref, lq2_ref, lk2_ref, g_ref, o_ref):
    t = q_ref.shape[0]
    qs = _stack_maps(q_ref[...].astype(BF16))
    pad = jnp.zeros((PAGE - t, V_DIM), F32)
    k = jnp.concatenate([k_ref[...], pad], axis=0).astype(BF16)
    v = jnp.concatenate([v_ref[...], pad], axis=0).astype(BF16)
    s = _dot_nt(qs, k)
    row = lax.broadcasted_iota(jnp.int32, s.shape, 0)
    col = lax.broadcasted_iota(jnp.int32, s.shape, 1)
    qrow = jnp.where(row >= t, row - t, row)
    s = jnp.where(col <= qrow, s, NEG_INF)
    p = jnp.exp(s - jnp.max(s, axis=-1, keepdims=True))
    o = _dot(p.astype(BF16), v) / jnp.sum(p, axis=-1, keepdims=True)
    lam = _lambda(lq1_ref, lk1_ref, lq2_ref, lk2_ref)
    o_ref[...] = _head_out(o[:t], o[t:], lam, g_ref[...])


def _meta_attn_call(q, k, v, lq1, lk1, lq2, lk2, subln_g):
    small = lambda shape: pl.BlockSpec(shape, lambda h: (0,) * len(shape))
    head = pl.BlockSpec((N_META, V_DIM), lambda h: (0, h))
    return pl.pallas_call(
        _meta_attn_kernel,
        grid=(HEADS,),
        in_specs=[head, head, head, small((1, HEAD_DIM)), small((1, HEAD_DIM)), small((1, HEAD_DIM)),
                  small((1, HEAD_DIM)), small((1, V_DIM))],
        out_specs=head,
        out_shape=jax.ShapeDtypeStruct((N_META, ATTN_W), F32),
        compiler_params=pltpu.CompilerParams(dimension_semantics=("arbitrary",)),
        name="meta_attn",
    )(q, k, v, lq1, lk1, lq2, lk2, subln_g)


def kernel(x_prompt, x_sample, cache_k, cache_v, state_conv, page_table, meta_tokens, ffn1_pre_g, ffn1_w_gate, ffn1_w_up, ffn1_w_down, ffn1_post_g, mix_pre_g, w_in, lambda_q1, lambda_k1, lambda_q2, lambda_k2, subln_g, conv_w, w_out, mix_post_g, ffn2_pre_g, ffn2_w_gate, ffn2_w_up, ffn2_w_down, ffn2_post_g):
    batch, seq, _ = x_prompt.shape
    n_seq, t_new, _ = x_sample.shape
    n_pool = cache_k.shape[1]
    l = 0

    bf = lambda w: w[l].astype(BF16)
    f1 = (ffn1_pre_g, bf(ffn1_w_gate), bf(ffn1_w_up), bf(ffn1_w_down), ffn1_post_g)
    f2 = (ffn2_pre_g, bf(ffn2_w_gate), bf(ffn2_w_up), bf(ffn2_w_down), ffn2_post_g)
    w_in_b, w_out_b = bf(w_in), bf(w_out)
    lam_args = (lambda_q1, lambda_k1, lambda_q2, lambda_k2, subln_g)
    ckt = jnp.transpose(cache_k[l], (0, 2, 3, 4, 1)).reshape(n_pool, ATTN_W, PAGE)
    cv = cache_v[l].reshape(n_pool, PAGE * HEADS, V_DIM)

    n_s = n_seq * t_new
    n_x = N_META + n_s
    xe = jnp.concatenate([meta_tokens, x_sample.reshape(n_s, D_MODEL)], axis=0)
    st = state_conv[l]
    zpad = lambda a, n: jnp.concatenate([a, jnp.zeros((n_seq, n, CONV_W), F32)], axis=1).reshape(n_s, CONV_W)
    zmeta = jnp.zeros((N_META, CONV_W), F32)
    p1 = jnp.concatenate([zmeta, zpad(st[:, 1:2], t_new - 1)], axis=0)
    p2 = jnp.concatenate([zmeta, zpad(st, t_new - 2)], axis=0)
    r = jnp.arange(n_x)[:, None]
    in_sample = r >= N_META
    pos = jnp.where(in_sample, (r - N_META) % t_new, r)
    m1 = jnp.broadcast_to((pos < 1).astype(F32), (n_x, CONV_W))
    m2 = jnp.broadcast_to((pos < 2).astype(F32), (n_x, CONV_W))

    xe1 = _ffn_stage_call(xe, f1, n_x, name="ffn1_extra")
    qe, ke, ve, ce, ue = _inproj_extra_call(xe1, mix_pre_g, w_in_b, conv_w[l], p1, p2, m1, m2)
    ae_meta = _meta_attn_call(qe[:N_META], ke[:N_META], ve[:N_META], *lam_args)

    rows = batch * seq
    half = n_seq // 2
    host = lambda seq0: (page_table, seq0, half, qe, ke, ve, N_META, t_new, ckt, cv, lam_args)
    xp = x_prompt.reshape(rows, D_MODEL)
    xp1, ae_s0 = _ffn_stage_call(xp, f1, HOST_ROW_TILE, host=host(0), name="ffn1")
    init = jnp.concatenate([jnp.zeros((SUBLANES - 2, CONV_W), F32), ue[N_META - 2:N_META]], axis=0)
    qp, kpt, vp4, kpb, vpb, cp, tail = _inproj_main_call(xp1, mix_pre_g, w_in_b, conv_w[l], init,
                                                        tile=ROW_TILE, rows_per_seq=seq)
    meta_pad = lambda a: jnp.concatenate([a[:N_META], jnp.zeros((PAGE - N_META, ATTN_W), F32)], axis=0).astype(BF16)
    shp = (batch, seq, ATTN_W)
    ap = _prompt_attn_call(qp.reshape(shp), kpb.reshape(shp), vpb.reshape(shp), meta_pad(ke), meta_pad(ve),
                           *lam_args)
    yp, ae_s1 = _ffn_stage_call(xp1, f2, HOST_ROW_TILE, mix=(ap.reshape(rows, ATTN_W), cp, w_out_b, mix_post_g),
                                host=host(half), name="mixout_ffn2")
    ae = jnp.concatenate([ae_meta, ae_s0, ae_s1], axis=0)
    ye = _ffn_stage_call(xe1, f2, n_x, mix=(ae, ce, w_out_b, mix_post_g), name="mixout_ffn2_extra")

    y_prompt = yp.reshape(batch, seq, D_MODEL)
    y_sample = ye[N_META:].reshape(n_seq, t_new, D_MODEL)
    kt_meta = jnp.broadcast_to(ke[:N_META].T[None], (batch, ATTN_W, N_META))
    kt_all = jnp.concatenate([kt_meta, kpt], axis=2).reshape(batch, HEADS, 2, HEAD_DIM, N_META + seq)
    k_prompt_new = jnp.transpose(kt_all, (0, 4, 1, 2, 3))[None]
    v_meta = jnp.broadcast_to(ve[:N_META].reshape(1, N_META, HEADS, V_DIM), (batch, N_META, HEADS, V_DIM))
    v_prompt_new = jnp.concatenate([v_meta, vp4.reshape(batch, seq, HEADS, V_DIM)], axis=1)[None]
    conv_prompt_new = tail[None]
    k_sample_new = ke[N_META:].reshape(1, n_seq, t_new, HEADS, 2, HEAD_DIM)
    v_sample_new = ve[N_META:].reshape(1, n_seq, t_new, HEADS, V_DIM)
    conv_sample_new = ue[N_META:].reshape(n_seq, t_new, CONV_W)[None, :, t_new - 2:, :]
    return (y_prompt, y_sample, k_prompt_new, v_prompt_new, conv_prompt_new,
            k_sample_new, v_sample_new, conv_sample_new)
```

```python
import functools
import math

import jax
import jax.numpy as jnp
from jax import lax
from jax.experimental import pallas as pl
from jax.experimental.pallas import tpu as pltpu

F32 = jnp.float32
BF16 = jnp.bfloat16

D_MODEL = 1024
D_FF = 2816
N_META = 16
HEADS = 4
HEAD_DIM = 64
V_DIM = 2 * HEAD_DIM
ATTN_W = HEADS * V_DIM
CONV_W = D_MODEL - ATTN_W
IN_COLS = 3 * ATTN_W + 3 * CONV_W
PAGE = 128
RMS_EPS = 1e-6
SUBLN_EPS = 1e-5
NEG_INF = -1e30
LAMBDA_INIT = 0.8 - 0.6 * math.exp(-0.3 * 0)
QK_SCALE = HEAD_DIM ** -0.5
LOG2E = math.log2(math.e)

SUBLANES = 8
LANES = 128
VMEM_LIMIT = 56 * 1024 * 1024
HOST_VMEM_LIMIT = 62 * 1024 * 1024

ROW_TILE = 512
HOST_ROW_TILE = 512
MXU_TILE = 256
FF_CHUNKS = ((0, 6 * MXU_TILE), (6 * MXU_TILE, D_FF))
KV_TILE = 256
Q_TILE = 2 * KV_TILE
KV_UNROLL = 8


def _rms(x, g, eps):
    ms = jnp.mean(x * x, axis=-1, keepdims=True)
    return x * lax.rsqrt(ms + eps) * g


def _dot(a, b):
    return jnp.dot(a, b, preferred_element_type=F32)


def _dot_nt(a, b):
    return lax.dot_general(a, b, (((1,), (1,)), ((), ())), preferred_element_type=F32)


def _ffn_half_step(x, pre_g, wg_ref, wu_ref, wd_ref, post_g, side_work=()):
    side_work = list(side_work)
    run_side = lambda: side_work.pop(0)() if side_work else None
    h = _rms(x, pre_g, RMS_EPS).astype(BF16)
    acc = None
    for lo, hi in FF_CHUNKS:
        sl = slice(lo, hi)
        g = _dot(h, wg_ref[:, sl])
        run_side()
        u = _dot(h, wu_ref[:, sl])
        run_side()
        a = (g * jax.nn.sigmoid(g) * u).astype(BF16)
        d = _dot(a, wd_ref[sl, :])
        run_side()
        acc = d if acc is None else acc + d
    assert not side_work
    return x + 0.5 * _rms(acc, post_g, RMS_EPS)


def _lambda(lq1_ref, lk1_ref, lq2_ref, lk2_ref):
    a = jnp.exp(jnp.sum(lq1_ref[...] * lk1_ref[...], axis=-1, keepdims=True))
    b = jnp.exp(jnp.sum(lq2_ref[...] * lk2_ref[...], axis=-1, keepdims=True))
    return a - b + LAMBDA_INIT


def _head_out(o1, o2, lam, g):
    o = o1 - lam * o2
    return _rms(o, g, SUBLN_EPS) * (1.0 - LAMBDA_INIT)


class _SampleAttn:
    N_REFS = 8

    def __init__(self, in_refs, k_refs, v_refs, o_ref, qbd_sc, m_sc, l_sc, acc_sc):
        self.q_ref, self.kn_ref, self.vn_ref = in_refs[:3]
        self.lam_refs, self.g_ref = in_refs[3:7], in_refs[7]
        self.k_refs, self.v_refs, self.o_ref = k_refs, v_refs, o_ref
        self.qbd_sc, self.m_sc, self.l_sc, self.acc_sc = qbd_sc, m_sc, l_sc, acc_sc
        self.t = self.q_ref.shape[0]

    def _softmax(self, s):
        m_prev = self.m_sc[...]
        m_new = jnp.maximum(m_prev, jnp.max(s, axis=1, keepdims=True))
        alpha = jnp.exp(m_prev - m_new)
        p = jnp.exp(s - m_new)
        self.l_sc[...] = alpha * self.l_sc[...] + jnp.sum(p, axis=1, keepdims=True)
        self.m_sc[...] = m_new
        return alpha, p.astype(BF16)

    def _values(self, g, alpha, pb, v_pair):
        t, acc_sc = self.t, self.acc_sc
        pv = _dot(pb[g * 4 * t:(g + 1) * 4 * t], v_pair)
        for e in range(2):
            rs = slice((2 * g + e) * 2 * t, (2 * g + e + 1) * 2 * t)
            acc_sc[rs, :] = (alpha[rs] * acc_sc[rs, :]
                             + pv[e * 2 * t:(e + 1) * 2 * t, e * V_DIM:(e + 1) * V_DIM])

    def start(self, j):
        t = self.t
        n_rows = HEADS * 2 * t

        @pl.when(j == 0)
        def _():
            qt = jnp.concatenate([self.q_ref[...]] * (n_rows // t), axis=0)
            row = lax.broadcasted_iota(jnp.int32, qt.shape, 0)
            col = lax.broadcasted_iota(jnp.int32, qt.shape, 1)
            self.qbd_sc[...] = jnp.where(col // HEAD_DIM == row // t, qt, 0.0).astype(BF16)
            self.m_sc[...] = jnp.full(self.m_sc.shape, NEG_INF, F32)
            self.l_sc[...] = jnp.zeros(self.l_sc.shape, F32)
            self.acc_sc[...] = jnp.zeros(self.acc_sc.shape, F32)

    def page_work(self, slots, kb_sc, vb_sc):
        st = {}

        def cast():
            for n, r in enumerate(slots):
                cols = slice(n * PAGE, (n + 1) * PAGE)
                kb_sc[:, cols] = self.k_refs[r][...].astype(BF16)
                for h in range(HEADS):
                    vb_sc[h // 2, cols, (h % 2) * V_DIM:(h % 2 + 1) * V_DIM] = (
                        self.v_refs[r][pl.ds(h, PAGE, stride=HEADS), :].astype(BF16))

        def scores():
            st["s"] = _dot(self.qbd_sc[...], kb_sc[...])

        def values():
            alpha, pb = self._softmax(st["s"])
            for g in range(HEADS // 2):
                self._values(g, alpha, pb, vb_sc[g])

        return cast, scores, values

    def finish(self, j, n_steps):
        t = self.t

        @pl.when(j == n_steps - 1)
        def _():
            pad = jnp.zeros((PAGE - t, ATTN_W), F32)
            kn = jnp.concatenate([self.kn_ref[...], pad], axis=0).astype(BF16)
            vn = jnp.concatenate([self.vn_ref[...], pad], axis=0).astype(BF16)
            s = _dot_nt(self.qbd_sc[...], kn)
            row = lax.broadcasted_iota(jnp.int32, s.shape, 0)
            key = lax.broadcasted_iota(jnp.int32, s.shape, 1)
            s = jnp.where((key < t) & (key <= row % t), s, NEG_INF)
            alpha, pb = self._softmax(s)
            for g in range(HEADS // 2):
                self._values(g, alpha, pb, vn[:, g * 2 * V_DIM:(g + 1) * 2 * V_DIM])
            o = self.acc_sc[...] / self.l_sc[...]
            lam = _lambda(*self.lam_refs)
            for h in range(HEADS):
                r0 = h * 2 * t
                self.o_ref[:, h * V_DIM:(h + 1) * V_DIM] = _head_out(
                    o[r0:r0 + t], o[r0 + t:r0 + 2 * t], lam, self.g_ref[...])


def _ffn_stage_kernel(has_mix, host, *refs):
    refs = list(refs)
    take = lambda n: [refs.pop(0) for _ in range(n)]
    if host:
        (pt_ref,) = take(1)
    (x_ref,) = take(1)
    if has_mix:
        a_ref, c_ref, wo_ref, mpost_ref = take(4)
    pre_ref, wg_ref, wu_ref, wd_ref, post_ref = take(5)
    if host:
        sample_refs = take(_SampleAttn.N_REFS)
        kt_hbm, v_hbm = take(2)
    (o_ref,) = take(1)
    side_work = ()
    if host:
        pps, host_steps, seq0, n_pages = host
        ao_ref, kbuf, vbuf, sems, kb_sc, vb_sc = take(6)
        i = pl.program_id(0)
        last = pl.num_programs(0) - 1
        j = i % host_steps
        halves = (range(0, pps // 2), range(pps // 2, pps))

        def copies(step, slots):
            first_page = (seq0 + step // host_steps) * n_pages + (step % host_steps) * pps
            out = []
            for r in slots:
                page = pt_ref[first_page + r]
                out.append(pltpu.make_async_copy(kt_hbm.at[page], kbuf.at[r], sems.at[r]))
                out.append(pltpu.make_async_copy(v_hbm.at[page], vbuf.at[r], sems.at[pps + r]))
            return out

        start = lambda cs: [c.start() for c in cs]
        wait = lambda cs: [c.wait() for c in cs]
        sample = _SampleAttn(sample_refs, [kbuf.at[r] for r in range(pps)], [vbuf.at[r] for r in range(pps)],
                             ao_ref, *refs)

        @pl.when(i == 0)
        def _():
            start(copies(i, halves[0]))

        sample.start(j)
        first_half, second_half = copies(i, halves[0]), copies(i, halves[1])
        next_first_half = copies(jnp.minimum(i + 1, last), halves[0])
        start(second_half)
        cast_a, scores_a, values_a = sample.page_work(halves[0], kb_sc, vb_sc)
        cast_b, scores_b, values_b = sample.page_work(halves[1], kb_sc, vb_sc)
        wait(first_half)
        cast_a()
        start(next_first_half)
        side_work = [scores_a, values_a, lambda: (wait(second_half), cast_b()), scores_b, values_b]
    x = x_ref[...]
    if has_mix:
        m = _dot(a_ref[...].astype(BF16), wo_ref[:ATTN_W, :]) + _dot(c_ref[...], wo_ref[ATTN_W:, :])
        x = x + _rms(m, mpost_ref[...], RMS_EPS)
    o_ref[...] = _ffn_half_step(x, pre_ref[...], wg_ref, wu_ref, wd_ref, post_ref[...], side_work)
    if host:
        sample.finish(j, host_steps)

        @pl.when(i == last)
        def _():
            wait(next_first_half)


def _const_spec(shape):
    return pl.BlockSpec(shape, lambda *_: (0,) * len(shape), pipeline_mode=pl.Buffered(1))


def _row_spec(tile, cols):
    return pl.BlockSpec((tile, cols), lambda i, *_: (i, 0))


def _ffn_weight_specs():
    return [_const_spec((1, D_MODEL)), _const_spec((D_MODEL, D_FF)), _const_spec((D_MODEL, D_FF)),
            _const_spec((D_FF, D_MODEL)), _const_spec((1, D_MODEL))]


def _row_params(vmem_limit=VMEM_LIMIT):
    return pltpu.CompilerParams(dimension_semantics=("arbitrary",), vmem_limit_bytes=vmem_limit)


def _ffn_stage_call(x, ffn, tile, mix=None, host=None, name="ffn"):
    rows = x.shape[0]
    steps = rows // tile
    args = [x]
    in_specs = [_row_spec(tile, D_MODEL)]
    if mix is not None:
        args += list(mix)
        in_specs += [_row_spec(tile, ATTN_W), _row_spec(tile, CONV_W),
                     _const_spec((D_MODEL, D_MODEL)), _const_spec((1, D_MODEL))]
    args += list(ffn)
    in_specs += _ffn_weight_specs()
    out_specs = _row_spec(tile, D_MODEL)
    out_shape = jax.ShapeDtypeStruct((rows, D_MODEL), F32)
    if host is None:
        return pl.pallas_call(
            functools.partial(_ffn_stage_kernel, mix is not None, None),
            grid=(steps,), in_specs=in_specs, out_specs=out_specs, out_shape=out_shape,
            compiler_params=_row_params(), name=name,
        )(*args)

    page_table, seq0, n_host_seq, q, k_new, v_new, row0, t, cache_kt, cache_v, lam_args = host
    n_pages = page_table.shape[1]
    assert steps % n_host_seq == 0 and row0 % t == 0
    host_steps = steps // n_host_seq
    pps = n_pages // host_steps
    assert pps * host_steps == n_pages
    blk0 = row0 // t + seq0
    n_rows = HEADS * 2 * t
    seq_spec = pl.BlockSpec((t, ATTN_W), lambda i, pt: (blk0 + i // host_steps, 0))
    small = lambda shape: pl.BlockSpec(shape, lambda i, pt: (0,) * len(shape))
    in_hbm = pl.BlockSpec(memory_space=pl.ANY)

    args += [q, k_new, v_new, *lam_args, cache_kt, cache_v]
    in_specs += [seq_spec] * 3 + [small((1, HEAD_DIM))] * 4 + [small((1, V_DIM))] + [in_hbm] * 2
    grid_spec = pltpu.PrefetchScalarGridSpec(
        num_scalar_prefetch=1,
        grid=(steps,),
        in_specs=in_specs,
        out_specs=[out_specs, pl.BlockSpec((t, ATTN_W), lambda i, pt: (i // host_steps, 0))],
        scratch_shapes=[pltpu.VMEM((pps, ATTN_W, PAGE), F32), pltpu.VMEM((pps, PAGE * HEADS, V_DIM), F32),
                        pltpu.SemaphoreType.DMA((2 * pps,)),
                        pltpu.VMEM((ATTN_W, pps // 2 * PAGE), BF16),
                        pltpu.VMEM((HEADS // 2, pps // 2 * PAGE, 2 * V_DIM), BF16),
                        pltpu.VMEM((n_rows, ATTN_W), BF16), pltpu.VMEM((n_rows, 1), F32),
                        pltpu.VMEM((n_rows, 1), F32), pltpu.VMEM((n_rows, V_DIM), F32)],
    )
    return pl.pallas_call(
        functools.partial(_ffn_stage_kernel, mix is not None, (pps, host_steps, seq0, n_pages)),
        grid_spec=grid_spec,
        out_shape=[out_shape, jax.ShapeDtypeStruct((n_host_seq * t, ATTN_W), F32)],
        compiler_params=_row_params(HOST_VMEM_LIMIT), name=name,
    )(page_table.reshape(-1), *args)


def _in_projection(x, g, w_ref, q_scale):
    h = _rms(x, g, RMS_EPS).astype(BF16)
    p = _dot(h, w_ref[...])
    q = p[:, :ATTN_W] * q_scale
    k = p[:, ATTN_W:2 * ATTN_W]
    v = p[:, 2 * ATTN_W:3 * ATTN_W]
    o = 3 * ATTN_W
    gate_b = p[:, o:o + CONV_W]
    u = p[:, o + CONV_W:o + 2 * CONV_W] * p[:, o + 2 * CONV_W:o + 3 * CONV_W]
    return q, k, v, gate_b, u


def _inproj_main_kernel(tiles_per_seq, x_ref, g_ref, w_ref, cw_ref, init_ref,
                        q_ref, kt_ref, v4_ref, kb_ref, vb_ref, c_ref, tail_ref, ubuf):
    tile = x_ref.shape[0]

    @pl.when(pl.program_id(0) % tiles_per_seq == 0)
    def _():
        ubuf[0:SUBLANES, :] = init_ref[...]

    q, k, v, gate_b, u = _in_projection(x_ref[...], g_ref[...], w_ref, QK_SCALE * LOG2E)
    q_ref[...] = q.astype(BF16)
    kt_ref[...] = k.T
    for h in range(HEADS):
        v4_ref[pl.ds(h, tile, stride=HEADS), :] = v[:, h * V_DIM:(h + 1) * V_DIM]
    kb_ref[...] = k.astype(BF16)
    vb_ref[...] = v.astype(BF16)
    ubuf[SUBLANES:SUBLANES + tile, :] = u
    u1 = ubuf[SUBLANES - 1:SUBLANES - 1 + tile, :]
    u2 = ubuf[SUBLANES - 2:SUBLANES - 2 + tile, :]
    y = cw_ref[0:1, :] * u2 + cw_ref[1:2, :] * u1 + cw_ref[2:3, :] * u
    c_ref[...] = (gate_b * y).astype(BF16)
    tail_ref[...] = u[tile - 2:, :]
    ubuf[0:SUBLANES, :] = u[tile - SUBLANES:, :]


def _inproj_extra_kernel(x_ref, g_ref, w_ref, cw_ref, p1_ref, p2_ref, m1_ref, m2_ref,
                         q_ref, kf_ref, vf_ref, c_ref, u_ref, ubuf):
    tile = x_ref.shape[0]
    q, k, v, gate_b, u = _in_projection(x_ref[...], g_ref[...], w_ref, QK_SCALE)
    q_ref[...] = q
    kf_ref[...] = k
    vf_ref[...] = v
    u_ref[...] = u
    ubuf[0:SUBLANES, :] = jnp.zeros((SUBLANES, CONV_W), F32)
    ubuf[SUBLANES:SUBLANES + tile, :] = u
    u1 = jnp.where(m1_ref[...] > 0.5, p1_ref[...], ubuf[SUBLANES - 1:SUBLANES - 1 + tile, :])
    u2 = jnp.where(m2_ref[...] > 0.5, p2_ref[...], ubuf[SUBLANES - 2:SUBLANES - 2 + tile, :])
    y = cw_ref[0:1, :] * u2 + cw_ref[1:2, :] * u1 + cw_ref[2:3, :] * u
    c_ref[...] = (gate_b * y).astype(BF16)


def _inproj_main_call(x1, g, w_in, conv_w, init, tile, rows_per_seq):
    rows = x1.shape[0]
    tiles_per_seq = rows_per_seq // tile
    n_seq = rows // rows_per_seq
    sd = jax.ShapeDtypeStruct
    return pl.pallas_call(
        functools.partial(_inproj_main_kernel, tiles_per_seq),
        grid=(rows // tile,),
        in_specs=[_row_spec(tile, D_MODEL), _const_spec((1, D_MODEL)), _const_spec((D_MODEL, IN_COLS)),
                  _const_spec((3, CONV_W)), _const_spec((SUBLANES, CONV_W))],
        out_specs=[_row_spec(tile, ATTN_W),
                   pl.BlockSpec((None, ATTN_W, tile), lambda i: (i // tiles_per_seq, 0, i % tiles_per_seq)),
                   _row_spec(tile * HEADS, V_DIM),
                   _row_spec(tile, ATTN_W), _row_spec(tile, ATTN_W), _row_spec(tile, CONV_W),
                   pl.BlockSpec((None, 2, CONV_W), lambda i: (i // tiles_per_seq, 0, 0))],
        out_shape=[sd((rows, ATTN_W), BF16), sd((n_seq, ATTN_W, rows_per_seq), F32),
                   sd((rows * HEADS, V_DIM), F32),
                   sd((rows, ATTN_W), BF16), sd((rows, ATTN_W), BF16), sd((rows, CONV_W), BF16),
                   sd((n_seq, 2, CONV_W), F32)],
        scratch_shapes=[pltpu.VMEM((SUBLANES + tile, CONV_W), F32)],
        compiler_params=_row_params(),
        name="inproj_main",
    )(x1, g, w_in, conv_w, init)


def _inproj_extra_call(x1, g, w_in, conv_w, p1, p2, m1, m2):
    rows = x1.shape[0]
    sd = jax.ShapeDtypeStruct
    full = lambda cols: pl.BlockSpec((rows, cols), lambda i: (0, 0))
    return pl.pallas_call(
        _inproj_extra_kernel,
        grid=(1,),
        in_specs=[full(D_MODEL), _const_spec((1, D_MODEL)), _const_spec((D_MODEL, IN_COLS)),
                  _const_spec((3, CONV_W))] + [full(CONV_W)] * 4,
        out_specs=[full(ATTN_W)] * 3 + [full(CONV_W)] * 2,
        out_shape=[sd((rows, ATTN_W), F32)] * 3 + [sd((rows, CONV_W), BF16), sd((rows, CONV_W), F32)],
        scratch_shapes=[pltpu.VMEM((SUBLANES + rows, CONV_W), F32)],
        compiler_params=_row_params(),
        name="inproj_extra",
    )(x1, g, w_in, conv_w, p1, p2, m1, m2)


def _stack_maps(q):
    lane = lax.broadcasted_iota(jnp.int32, q.shape, 1)
    zero = jnp.zeros_like(q)
    return jnp.concatenate([jnp.where(lane < HEAD_DIM, q, zero), jnp.where(lane >= HEAD_DIM, q, zero)], axis=0)


def _prompt_attn_kernel(q_ref, k_ref, v_ref, km_ref, vm_ref, lq1_ref, lk1_ref, lq2_ref, lk2_ref, g_ref,
                        o_ref, m_sc, acc_sc):
    tq = q_ref.shape[0]
    i = pl.program_id(1)
    qs = [_stack_maps(q_ref[:, h * V_DIM:(h + 1) * V_DIM]) for h in range(HEADS)]
    all_rows = [(slice(0, 2 * tq), None)]
    half_rows = [tuple(slice(m * tq + e * KV_TILE, m * tq + (e + 1) * KV_TILE) for m in range(2)) for e in range(2)]

    def update(h, rows, s, v_aug, first):
        m_cur = jnp.max(s, axis=1, keepdims=True)
        if first:
            m_new = jnp.broadcast_to(m_cur, (s.shape[0], LANES))
            p = jnp.exp2(s - m_cur)
            acc_sc[h, rows, :] = _dot(p.astype(BF16), v_aug)
        else:
            m_prev = m_sc[h, rows, :]
            m_new = jnp.maximum(m_prev, m_cur)
            alpha = jnp.exp2(m_prev - m_new)
            reps = s.shape[1] // LANES
            p = jnp.exp2(s - jnp.concatenate([m_new] * reps, axis=1))
            acc_sc[h, rows, :] = (jnp.concatenate([alpha, alpha], axis=1) * acc_sc[h, rows, :]
                                  + _dot(p.astype(BF16), v_aug))
        m_sc[h, rows, :] = m_new

    def block(k_blk, v_blk, row_groups, first):
        for h in range(HEADS):
            k = k_blk(h)
            v_aug = jnp.concatenate([v_blk(h), jnp.ones(k.shape, BF16)], axis=1)
            for rows, mask in row_groups:
                s = _dot_nt(qs[h][rows], k)
                if mask is not None:
                    s = jnp.where(mask, s, NEG_INF)
                update(h, rows, s, v_aug, first)

    def kv_block(j):
        start = pl.multiple_of(j * KV_TILE, KV_TILE)
        return (lambda h: k_ref[pl.ds(start, KV_TILE), h * V_DIM:(h + 1) * V_DIM],
                lambda h: v_ref[pl.ds(start, KV_TILE), h * V_DIM:(h + 1) * V_DIM])

    tiles_per_q = tq // KV_TILE
    n_full = i * tiles_per_q
    k_a, v_a = kv_block(n_full)
    row = lax.broadcasted_iota(jnp.int32, (KV_TILE, KV_TILE + PAGE), 0)
    col = lax.broadcasted_iota(jnp.int32, (KV_TILE, KV_TILE + PAGE), 1)
    meta_ok = (col >= KV_TILE) & (col < KV_TILE + N_META)
    first_masks = [((col <= row + e * KV_TILE) & (col < KV_TILE)) | meta_ok for e in range(2)]
    block(lambda h: jnp.concatenate([k_a(h), km_ref[:, h * V_DIM:(h + 1) * V_DIM]], axis=0),
          lambda h: jnp.concatenate([v_a(h), vm_ref[:, h * V_DIM:(h + 1) * V_DIM]], axis=0),
          [(rows, first_masks[e]) for e in range(2) for rows in half_rows[e]], True)
    causal = (lax.broadcasted_iota(jnp.int32, (KV_TILE, KV_TILE), 1)
              <= lax.broadcasted_iota(jnp.int32, (KV_TILE, KV_TILE), 0))
    block(*kv_block(n_full + 1), [(rows, causal) for rows in half_rows[1]], False)

    rem = n_full % KV_UNROLL
    width = tiles_per_q
    while width < KV_UNROLL:
        def leftover(width=width):
            first_blk = (n_full // KV_UNROLL) * KV_UNROLL + (rem & (width - 1))
            for d in range(width):
                block(*kv_block(first_blk + d), all_rows, False)
        pl.when(rem & width != 0)(leftover)
        width *= 2

    def full_group(jj, carry):
        for d in range(KV_UNROLL):
            block(*kv_block(KV_UNROLL * jj + d), all_rows, False)
        return carry

    lax.fori_loop(0, n_full // KV_UNROLL, full_group, 0)

    lam = _lambda(lq1_ref, lk1_ref, lq2_ref, lk2_ref)
    for h in range(HEADS):
        acc = acc_sc[h]
        o = acc[:, :V_DIM] / acc[:, V_DIM:]
        o_ref[:, h * V_DIM:(h + 1) * V_DIM] = _head_out(o[:tq], o[tq:], lam, g_ref[...]).astype(o_ref.dtype)


def _prompt_attn_call(q, k, v, km, vm, lq1, lk1, lq2, lk2, subln_g):
    b, s, _ = q.shape
    small = lambda shape: pl.BlockSpec(shape, lambda b_, i: (0,) * len(shape))
    resident = pl.BlockSpec((None, s, ATTN_W), lambda b_, i: (b_, 0, 0))
    return pl.pallas_call(
        _prompt_attn_kernel,
        grid=(b, s // Q_TILE),
        in_specs=[pl.BlockSpec((None, Q_TILE, ATTN_W), lambda b_, i: (b_, i, 0)), resident, resident,
                  small((PAGE, ATTN_W)), small((PAGE, ATTN_W)),
                  small((1, HEAD_DIM)), small((1, HEAD_DIM)), small((1, HEAD_DIM)), small((1, HEAD_DIM)),
                  small((1, V_DIM))],
        out_specs=pl.BlockSpec((None, Q_TILE, ATTN_W), lambda b_, i: (b_, i, 0)),
        out_shape=jax.ShapeDtypeStruct((b, s, ATTN_W), BF16),
        scratch_shapes=[pltpu.VMEM((HEADS, 2 * Q_TILE, LANES), F32),
                        pltpu.VMEM((HEADS, 2 * Q_TILE, 2 * V_DIM), F32)],
        compiler_params=pltpu.CompilerParams(dimension_semantics=("arbitrary",) * 2,
                                             vmem_limit_bytes=VMEM_LIMIT),
        name="prompt_attn",
    )(q, k, v, km, vm, lq1, lk1, lq2, lk2, subln_g)


def _meta_attn_kernel(q_ref, k_ref, v_ref, lq1_ref, lk1_ref, lq2_ref, lk2_ref, g_ref, o_ref):
    t = q_ref.shape[0]
    qs = _stack_maps(q_ref[...].astype(BF16))
    pad = jnp.zeros((PAGE - t, V_DIM), F32)
    k = jnp.concatenate([k_ref[...], pad], axis=0).astype(BF16)
    v = jnp.concatenate([v_ref[...], pad], axis=0).astype(BF16)
    s = _dot_nt(qs, k)
    row = lax.broadcasted_iota(jnp.int32, s.shape, 0)
    col = lax.broadcasted_iota(jnp.int32, s.shape, 1)
    qrow = jnp.where(row >= t, row - t, row)
    s = jnp.where(col <= qrow, s, NEG_INF)
    p = jnp.exp(s - jnp.max(s, axis=-1, keepdims=True))
    o = _dot(p.astype(BF16), v) / jnp.sum(p, axis=-1, keepdims=True)
    lam = _lambda(lq1_ref, lk1_ref, lq2_ref, lk2_ref)
    o_ref[...] = _head_out(o[:t], o[t:], lam, g_ref[...])


def _meta_attn_call(q, k, v, lq1, lk1, lq2, lk2, subln_g):
    small = lambda shape: pl.BlockSpec(shape, lambda h: (0,) * len(shape))
    head = pl.BlockSpec((N_META, V_DIM), lambda h: (0, h))
    return pl.pallas_call(
        _meta_attn_kernel,
        grid=(HEADS,),
        in_specs=[head, head, head, small((1, HEAD_DIM)), small((1, HEAD_DIM)), small((1, HEAD_DIM)),
                  small((1, HEAD_DIM)), small((1, V_DIM))],
        out_specs=head,
        out_shape=jax.ShapeDtypeStruct((N_META, ATTN_W), F32),
        compiler_params=pltpu.CompilerParams(dimension_semantics=("arbitrary",)),
        name="meta_attn",
    )(q, k, v, lq1, lk1, lq2, lk2, subln_g)


def kernel(x_prompt, x_sample, cache_k, cache_v, state_conv, page_table, meta_tokens, ffn1_pre_g, ffn1_w_gate, ffn1_w_up, ffn1_w_down, ffn1_post_g, mix_pre_g, w_in, lambda_q1, lambda_k1, lambda_q2, lambda_k2, subln_g, conv_w, w_out, mix_post_g, ffn2_pre_g, ffn2_w_gate, ffn2_w_up, ffn2_w_down, ffn2_post_g):
    batch, seq, _ = x_prompt.shape
    n_seq, t_new, _ = x_sample.shape
    n_pool = cache_k.shape[1]
    l = 0

    bf = lambda w: w[l].astype(BF16)
    f1 = (ffn1_pre_g, bf(ffn1_w_gate), bf(ffn1_w_up), bf(ffn1_w_down), ffn1_post_g)
    f2 = (ffn2_pre_g, bf(ffn2_w_gate), bf(ffn2_w_up), bf(ffn2_w_down), ffn2_post_g)
    w_in_b, w_out_b = bf(w_in), bf(w_out)
    lam_args = (lambda_q1, lambda_k1, lambda_q2, lambda_k2, subln_g)
    ckt = jnp.transpose(cache_k[l], (0, 2, 3, 4, 1)).reshape(n_pool, ATTN_W, PAGE)
    cv = cache_v[l].reshape(n_pool, PAGE * HEADS, V_DIM)

    n_s = n_seq * t_new
    n_x = N_META + n_s
    xe = jnp.concatenate([meta_tokens, x_sample.reshape(n_s, D_MODEL)], axis=0)
    st = state_conv[l]
    zpad = lambda a, n: jnp.concatenate([a, jnp.zeros((n_seq, n, CONV_W), F32)], axis=1).reshape(n_s, CONV_W)
    zmeta = jnp.zeros((N_META, CONV_W), F32)
    p1 = jnp.concatenate([zmeta, zpad(st[:, 1:2], t_new - 1)], axis=0)
    p2 = jnp.concatenate([zmeta, zpad(st, t_new - 2)], axis=0)
    r = jnp.arange(n_x)[:, None]
    in_sample = r >= N_META
    pos = jnp.where(in_sample, (r - N_META) % t_new, r)
    m1 = jnp.broadcast_to((pos < 1).astype(F32), (n_x, CONV_W))
    m2 = jnp.broadcast_to((pos < 2).astype(F32), (n_x, CONV_W))

    xe1 = _ffn_stage_call(xe, f1, n_x, name="ffn1_extra")
    qe, ke, ve, ce, ue = _inproj_extra_call(xe1, mix_pre_g, w_in_b, conv_w[l], p1, p2, m1, m2)
    ae_meta = _meta_attn_call(qe[:N_META], ke[:N_META], ve[:N_META], *lam_args)

    rows = batch * seq
    half = n_seq // 2
    host = lambda seq0: (page_table, seq0, half, qe, ke, ve, N_META, t_new, ckt, cv, lam_args)
    xp = x_prompt.reshape(rows, D_MODEL)
    xp1, ae_s0 = _ffn_stage_call(xp, f1, HOST_ROW_TILE, host=host(0), name="ffn1")
    init = jnp.concatenate([jnp.zeros((SUBLANES - 2, CONV_W), F32), ue[N_META - 2:N_META]], axis=0)
    qp, kpt, vp4, kpb, vpb, cp, tail = _inproj_main_call(xp1, mix_pre_g, w_in_b, conv_w[l], init,
                                                        tile=ROW_TILE, rows_per_seq=seq)
    meta_pad = lambda a: jnp.concatenate([a[:N_META], jnp.zeros((PAGE - N_META, ATTN_W), F32)], axis=0).astype(BF16)
    shp = (batch, seq, ATTN_W)
    ap = _prompt_attn_call(qp.reshape(shp), kpb.reshape(shp), vpb.reshape(shp), meta_pad(ke), meta_pad(ve),
                           *lam_args)
    yp, ae_s1 = _ffn_stage_call(xp1, f2, HOST_ROW_TILE, mix=(ap.reshape(rows, ATTN_W), cp, w_out_b, mix_post_g),
                                host=host(half), name="mixout_ffn2")
    ae = jnp.concatenate([ae_meta, ae_s0, ae_s1], axis=0)
    ye = _ffn_stage_call(xe1, f2, n_x, mix=(ae, ce, w_out_b, mix_post_g), name="mixout_ffn2_extra")

    y_prompt = yp.reshape(batch, seq, D_MODEL)
    y_sample = ye[N_META:].reshape(n_seq, t_new, D_MODEL)
    kt_meta = jnp.broadcast_to(ke[:N_META].T[None], (batch, ATTN_W, N_META))
    kt_all = jnp.concatenate([kt_meta, kpt], axis=2).reshape(batch, HEADS, 2, HEAD_DIM, N_META + seq)
    k_prompt_new = jnp.transpose(kt_all, (0, 4, 1, 2, 3))[None]
    v_meta = jnp.broadcast_to(ve[:N_META].reshape(1, N_META, HEADS, V_DIM), (batch, N_META, HEADS, V_DIM))
    v_prompt_new = jnp.concatenate([v_meta, vp4.reshape(batch, seq, HEADS, V_DIM)], axis=1)[None]
    conv_prompt_new = tail[None]
    k_sample_new = ke[N_META:].reshape(1, n_seq, t_new, HEADS, 2, HEAD_DIM)
    v_sample_new = ve[N_META:].reshape(1, n_seq, t_new, HEADS, V_DIM)
    conv_sample_new = ue[N_META:].reshape(n_seq, t_new, CONV_W)[None, :, t_new - 2:, :]
    return (y_prompt, y_sample, k_prompt_new, v_prompt_new, conv_prompt_new,
            k_sample_new, v_sample_new, conv_sample_new)
```

```python
import functools
import math

import jax
import jax.numpy as jnp
from jax import lax
from jax.experimental import pallas as pl
from jax.experimental.pallas import tpu as pltpu

F32 = jnp.float32
BF16 = jnp.bfloat16

D_MODEL = 1024
D_FF = 2816
N_META = 16
HEADS = 4
HEAD_DIM = 64
V_DIM = 2 * HEAD_DIM
ATTN_W = HEADS * V_DIM
CONV_W = D_MODEL - ATTN_W
IN_COLS = 3 * ATTN_W + 3 * CONV_W
PAGE = 128
RMS_EPS = 1e-6
SUBLN_EPS = 1e-5
NEG_INF = -1e30
LAMBDA_INIT = 0.8 - 0.6 * math.exp(-0.3 * 0)
QK_SCALE = HEAD_DIM ** -0.5
LOG2E = math.log2(math.e)

SUBLANES = 8
LANES = 128
VMEM_LIMIT = 56 * 1024 * 1024
HOST_VMEM_LIMIT = 62 * 1024 * 1024

ROW_TILE = 512
HOST_ROW_TILE = 512
MXU_TILE = 256
FF_CHUNKS = ((0, 6 * MXU_TILE), (6 * MXU_TILE, D_FF))
KV_TILE = 256
Q_TILE = 2 * KV_TILE
KV_UNROLL = 8


def _rms(x, g, eps):
    ms = jnp.mean(x * x, axis=-1, keepdims=True)
    return x * lax.rsqrt(ms + eps) * g


def _dot(a, b):
    return jnp.dot(a, b, preferred_element_type=F32)


def _dot_nt(a, b):
    return lax.dot_general(a, b, (((1,), (1,)), ((), ())), preferred_element_type=F32)


def _ffn_half_step(x, pre_g, wg_ref, wu_ref, wd_ref, post_g, side_work=()):
    side_work = list(side_work)
    run_side = lambda: side_work.pop(0)() if side_work else None
    h = _rms(x, pre_g, RMS_EPS).astype(BF16)
    acc = None
    for lo, hi in FF_CHUNKS:
        sl = slice(lo, hi)
        g = _dot(h, wg_ref[:, sl])
        run_side()
        u = _dot(h, wu_ref[:, sl])
        run_side()
        a = (g * jax.nn.sigmoid(g) * u).astype(BF16)
        d = _dot(a, wd_ref[sl, :])
        run_side()
        acc = d if acc is None else acc + d
    assert not side_work
    return x + 0.5 * _rms(acc, post_g, RMS_EPS)


def _lambda(lq1_ref, lk1_ref, lq2_ref, lk2_ref):
    a = jnp.exp(jnp.sum(lq1_ref[...] * lk1_ref[...], axis=-1, keepdims=True))
    b = jnp.exp(jnp.sum(lq2_ref[...] * lk2_ref[...], axis=-1, keepdims=True))
    return a - b + LAMBDA_INIT


def _head_out(o1, o2, lam, g):
    o = o1 - lam * o2
    return _rms(o, g, SUBLN_EPS) * (1.0 - LAMBDA_INIT)


class _SampleAttn:
    N_REFS = 8

    def __init__(self, in_refs, k_refs, v_refs, o_ref, qbd_sc, m_sc, l_sc, acc_sc):
        self.q_ref, self.kn_ref, self.vn_ref = in_refs[:3]
        self.lam_refs, self.g_ref = in_refs[3:7], in_refs[7]
        self.k_refs, self.v_refs, self.o_ref = k_refs, v_refs, o_ref
        self.qbd_sc, self.m_sc, self.l_sc, self.acc_sc = qbd_sc, m_sc, l_sc, acc_sc
        self.t = self.q_ref.shape[0]

    def _softmax(self, s):
        m_prev = self.m_sc[...]
        m_new = jnp.maximum(m_prev, jnp.max(s, axis=1, keepdims=True))
        alpha = jnp.exp(m_prev - m_new)
        p = jnp.exp(s - m_new)
        self.l_sc[...] = alpha * self.l_sc[...] + jnp.sum(p, axis=1, keepdims=True)
        self.m_sc[...] = m_new
        return alpha, p.astype(BF16)

    def _values(self, g, alpha, pb, v_pair):
        t, acc_sc = self.t, self.acc_sc
        pv = _dot(pb[g * 4 * t:(g + 1) * 4 * t], v_pair)
        for e in range(2):
            rs = slice((2 * g + e) * 2 * t, (2 * g + e + 1) * 2 * t)
            acc_sc[rs, :] = (alpha[rs] * acc_sc[rs, :]
                             + pv[e * 2 * t:(e + 1) * 2 * t, e * V_DIM:(e + 1) * V_DIM])

    def start(self, j):
        t = self.t
        n_rows = HEADS * 2 * t

        @pl.when(j == 0)
        def _():
            qt = jnp.concatenate([self.q_ref[...]] * (n_rows // t), axis=0)
            row = lax.broadcasted_iota(jnp.int32, qt.shape, 0)
            col = lax.broadcasted_iota(jnp.int32, qt.shape, 1)
            self.qbd_sc[...] = jnp.where(col // HEAD_DIM == row // t, qt, 0.0).astype(BF16)
            self.m_sc[...] = jnp.full(self.m_sc.shape, NEG_INF, F32)
            self.l_sc[...] = jnp.zeros(self.l_sc.shape, F32)
            self.acc_sc[...] = jnp.zeros(self.acc_sc.shape, F32)

    def page_work(self, slots, kb_sc, vb_sc):
        st = {}

        def cast():
            for n, r in enumerate(slots):
                cols = slice(n * PAGE, (n + 1) * PAGE)
                kb_sc[:, cols] = self.k_refs[r][...].astype(BF16)
                for h in range(HEADS):
                    vb_sc[h // 2, cols, (h % 2) * V_DIM:(h % 2 + 1) * V_DIM] = (
                        self.v_refs[r][pl.ds(h, PAGE, stride=HEADS), :].astype(BF16))

        def scores():
            st["s"] = _dot(self.qbd_sc[...], kb_sc[...])

        def values():
            alpha, pb = self._softmax(st["s"])
            for g in range(HEADS // 2):
                self._values(g, alpha, pb, vb_sc[g])

        return cast, scores, values

    def finish(self, j, n_steps):
        t = self.t

        @pl.when(j == n_steps - 1)
        def _():
            pad = jnp.zeros((PAGE - t, ATTN_W), F32)
            kn = jnp.concatenate([self.kn_ref[...], pad], axis=0).astype(BF16)
            vn = jnp.concatenate([self.vn_ref[...], pad], axis=0).astype(BF16)
            s = _dot_nt(self.qbd_sc[...], kn)
            row = lax.broadcasted_iota(jnp.int32, s.shape, 0)
            key = lax.broadcasted_iota(jnp.int32, s.shape, 1)
            s = jnp.where((key < t) & (key <= row % t), s, NEG_INF)
            alpha, pb = self._softmax(s)
            for g in range(HEADS // 2):
                self._values(g, alpha, pb, vn[:, g * 2 * V_DIM:(g + 1) * 2 * V_DIM])
            o = self.acc_sc[...] / self.l_sc[...]
            lam = _lambda(*self.lam_refs)
            for h in range(HEADS):
                r0 = h * 2 * t
                self.o_ref[:, h * V_DIM:(h + 1) * V_DIM] = _head_out(
                    o[r0:r0 + t], o[r0 + t:r0 + 2 * t], lam, self.g_ref[...])


def _ffn_stage_kernel(has_mix, host, *refs):
    refs = list(refs)
    take = lambda n: [refs.pop(0) for _ in range(n)]
    if host:
        (pt_ref,) = take(1)
    (x_ref,) = take(1)
    if has_mix:
        a_ref, c_ref, wo_ref, mpost_ref = take(4)
    pre_ref, wg_ref, wu_ref, wd_ref, post_ref = take(5)
    if host:
        sample_refs = take(_SampleAttn.N_REFS)
        kt_hbm, v_hbm = take(2)
    (o_ref,) = take(1)
    side_work = ()
    if host:
        pps, host_steps, seq0, n_pages = host
        ao_ref, kbuf, vbuf, sems, kb_sc, vb_sc = take(6)
        i = pl.program_id(0)
        last = pl.num_programs(0) - 1
        j = i % host_steps
        halves = (range(0, pps // 2), range(pps // 2, pps))

        def copies(step, slots):
            first_page = (seq0 + step // host_steps) * n_pages + (step % host_steps) * pps
            out = []
            for r in slots:
                page = pt_ref[first_page + r]
                out.append(pltpu.make_async_copy(kt_hbm.at[page], kbuf.at[r], sems.at[r]))
                out.append(pltpu.make_async_copy(v_hbm.at[page], vbuf.at[r], sems.at[pps + r]))
            return out

        start = lambda cs: [c.start(priority=n % 2) for n, c in enumerate(cs)]
        wait = lambda cs: [c.wait() for c in cs]
        sample = _SampleAttn(sample_refs, [kbuf.at[r] for r in range(pps)], [vbuf.at[r] for r in range(pps)],
                             ao_ref, *refs)

        @pl.when(i == 0)
        def _():
            start(copies(i, halves[0]))

        sample.start(j)
        first_half, second_half = copies(i, halves[0]), copies(i, halves[1])
        next_first_half = copies(jnp.minimum(i + 1, last), halves[0])
        start(second_half)
        cast_a, scores_a, values_a = sample.page_work(halves[0], kb_sc, vb_sc)
        cast_b, scores_b, values_b = sample.page_work(halves[1], kb_sc, vb_sc)
        wait(first_half)
        cast_a()
        start(next_first_half)
        side_work = [scores_a, values_a, lambda: (wait(second_half), cast_b()), scores_b, values_b]
    x = x_ref[...]
    if has_mix:
        m = _dot(a_ref[...].astype(BF16), wo_ref[:ATTN_W, :]) + _dot(c_ref[...], wo_ref[ATTN_W:, :])
        x = x + _rms(m, mpost_ref[...], RMS_EPS)
    o_ref[...] = _ffn_half_step(x, pre_ref[...], wg_ref, wu_ref, wd_ref, post_ref[...], side_work)
    if host:
        sample.finish(j, host_steps)

        @pl.when(i == last)
        def _():
            wait(next_first_half)


def _const_spec(shape):
    return pl.BlockSpec(shape, lambda *_: (0,) * len(shape), pipeline_mode=pl.Buffered(1))


def _row_spec(tile, cols):
    return pl.BlockSpec((tile, cols), lambda i, *_: (i, 0))


def _ffn_weight_specs():
    return [_const_spec((1, D_MODEL)), _const_spec((D_MODEL, D_FF)), _const_spec((D_MODEL, D_FF)),
            _const_spec((D_FF, D_MODEL)), _const_spec((1, D_MODEL))]


def _row_params(vmem_limit=VMEM_LIMIT):
    return pltpu.CompilerParams(dimension_semantics=("arbitrary",), vmem_limit_bytes=vmem_limit)


def _ffn_stage_call(x, ffn, tile, mix=None, host=None, name="ffn"):
    rows = x.shape[0]
    steps = rows // tile
    args = [x]
    in_specs = [_row_spec(tile, D_MODEL)]
    if mix is not None:
        args += list(mix)
        in_specs += [_row_spec(tile, ATTN_W), _row_spec(tile, CONV_W),
                     _const_spec((D_MODEL, D_MODEL)), _const_spec((1, D_MODEL))]
    args += list(ffn)
    in_specs += _ffn_weight_specs()
    out_specs = _row_spec(tile, D_MODEL)
    out_shape = jax.ShapeDtypeStruct((rows, D_MODEL), F32)
    if host is None:
        return pl.pallas_call(
            functools.partial(_ffn_stage_kernel, mix is not None, None),
            grid=(steps,), in_specs=in_specs, out_specs=out_specs, out_shape=out_shape,
            compiler_params=_row_params(), name=name,
        )(*args)

    page_table, seq0, n_host_seq, q, k_new, v_new, row0, t, cache_kt, cache_v, lam_args = host
    n_pages = page_table.shape[1]
    assert steps % n_host_seq == 0 and row0 % t == 0
    host_steps = steps // n_host_seq
    pps = n_pages // host_steps
    assert pps * host_steps == n_pages
    blk0 = row0 // t + seq0
    n_rows = HEADS * 2 * t
    seq_spec = pl.BlockSpec((t, ATTN_W), lambda i, pt: (blk0 + i // host_steps, 0))
    small = lambda shape: pl.BlockSpec(shape, lambda i, pt: (0,) * len(shape))
    in_hbm = pl.BlockSpec(memory_space=pl.ANY)

    args += [q, k_new, v_new, *lam_args, cache_kt, cache_v]
    in_specs += [seq_spec] * 3 + [small((1, HEAD_DIM))] * 4 + [small((1, V_DIM))] + [in_hbm] * 2
    grid_spec = pltpu.PrefetchScalarGridSpec(
        num_scalar_prefetch=1,
        grid=(steps,),
        in_specs=in_specs,
        out_specs=[out_specs, pl.BlockSpec((t, ATTN_W), lambda i, pt: (i // host_steps, 0))],
        scratch_shapes=[pltpu.VMEM((pps, ATTN_W, PAGE), F32), pltpu.VMEM((pps, PAGE * HEADS, V_DIM), F32),
                        pltpu.SemaphoreType.DMA((2 * pps,)),
                        pltpu.VMEM((ATTN_W, pps // 2 * PAGE), BF16),
                        pltpu.VMEM((HEADS // 2, pps // 2 * PAGE, 2 * V_DIM), BF16),
                        pltpu.VMEM((n_rows, ATTN_W), BF16), pltpu.VMEM((n_rows, 1), F32),
                        pltpu.VMEM((n_rows, 1), F32), pltpu.VMEM((n_rows, V_DIM), F32)],
    )
    return pl.pallas_call(
        functools.partial(_ffn_stage_kernel, mix is not None, (pps, host_steps, seq0, n_pages)),
        grid_spec=grid_spec,
        out_shape=[out_shape, jax.ShapeDtypeStruct((n_host_seq * t, ATTN_W), F32)],
        compiler_params=_row_params(HOST_VMEM_LIMIT), name=name,
    )(page_table.reshape(-1), *args)


def _in_projection(x, g, w_ref, q_scale):
    h = _rms(x, g, RMS_EPS).astype(BF16)
    p = _dot(h, w_ref[...])
    q = p[:, :ATTN_W] * q_scale
    k = p[:, ATTN_W:2 * ATTN_W]
    v = p[:, 2 * ATTN_W:3 * ATTN_W]
    o = 3 * ATTN_W
    gate_b = p[:, o:o + CONV_W]
    u = p[:, o + CONV_W:o + 2 * CONV_W] * p[:, o + 2 * CONV_W:o + 3 * CONV_W]
    return q, k, v, gate_b, u


def _inproj_main_kernel(tiles_per_seq, x_ref, g_ref, w_ref, cw_ref, init_ref,
                        q_ref, kt_ref, v4_ref, kb_ref, vb_ref, c_ref, tail_ref, ubuf):
    tile = x_ref.shape[0]

    @pl.when(pl.program_id(0) % tiles_per_seq == 0)
    def _():
        ubuf[0:SUBLANES, :] = init_ref[...]

    q, k, v, gate_b, u = _in_projection(x_ref[...], g_ref[...], w_ref, QK_SCALE * LOG2E)
    q_ref[...] = q.astype(BF16)
    kt_ref[...] = k.T
    for h in range(HEADS):
        v4_ref[pl.ds(h, tile, stride=HEADS), :] = v[:, h * V_DIM:(h + 1) * V_DIM]
    kb_ref[...] = k.astype(BF16)
    vb_ref[...] = v.astype(BF16)
    ubuf[SUBLANES:SUBLANES + tile, :] = u
    u1 = ubuf[SUBLANES - 1:SUBLANES - 1 + tile, :]
    u2 = ubuf[SUBLANES - 2:SUBLANES - 2 + tile, :]
    y = cw_ref[0:1, :] * u2 + cw_ref[1:2, :] * u1 + cw_ref[2:3, :] * u
    c_ref[...] = (gate_b * y).astype(BF16)
    tail_ref[...] = u[tile - 2:, :]
    ubuf[0:SUBLANES, :] = u[tile - SUBLANES:, :]


def _inproj_extra_kernel(x_ref, g_ref, w_ref, cw_ref, p1_ref, p2_ref, m1_ref, m2_ref,
                         q_ref, kf_ref, vf_ref, c_ref, u_ref, ubuf):
    tile = x_ref.shape[0]
    q, k, v, gate_b, u = _in_projection(x_ref[...], g_ref[...], w_ref, QK_SCALE)
    q_ref[...] = q
    kf_ref[...] = k
    vf_ref[...] = v
    u_ref[...] = u
    ubuf[0:SUBLANES, :] = jnp.zeros((SUBLANES, CONV_W), F32)
    ubuf[SUBLANES:SUBLANES + tile, :] = u
    u1 = jnp.where(m1_ref[...] > 0.5, p1_ref[...], ubuf[SUBLANES - 1:SUBLANES - 1 + tile, :])
    u2 = jnp.where(m2_ref[...] > 0.5, p2_ref[...], ubuf[SUBLANES - 2:SUBLANES - 2 + tile, :])
    y = cw_ref[0:1, :] * u2 + cw_ref[1:2, :] * u1 + cw_ref[2:3, :] * u
    c_ref[...] = (gate_b * y).astype(BF16)


def _inproj_main_call(x1, g, w_in, conv_w, init, tile, rows_per_seq):
    rows = x1.shape[0]
    tiles_per_seq = rows_per_seq // tile
    n_seq = rows // rows_per_seq
    sd = jax.ShapeDtypeStruct
    return pl.pallas_call(
        functools.partial(_inproj_main_kernel, tiles_per_seq),
        grid=(rows // tile,),
        in_specs=[_row_spec(tile, D_MODEL), _const_spec((1, D_MODEL)), _const_spec((D_MODEL, IN_COLS)),
                  _const_spec((3, CONV_W)), _const_spec((SUBLANES, CONV_W))],
        out_specs=[_row_spec(tile, ATTN_W),
                   pl.BlockSpec((None, ATTN_W, tile), lambda i: (i // tiles_per_seq, 0, i % tiles_per_seq)),
                   _row_spec(tile * HEADS, V_DIM),
                   _row_spec(tile, ATTN_W), _row_spec(tile, ATTN_W), _row_spec(tile, CONV_W),
                   pl.BlockSpec((None, 2, CONV_W), lambda i: (i // tiles_per_seq, 0, 0))],
        out_shape=[sd((rows, ATTN_W), BF16), sd((n_seq, ATTN_W, rows_per_seq), F32),
                   sd((rows * HEADS, V_DIM), F32),
                   sd((rows, ATTN_W), BF16), sd((rows, ATTN_W), BF16), sd((rows, CONV_W), BF16),
                   sd((n_seq, 2, CONV_W), F32)],
        scratch_shapes=[pltpu.VMEM((SUBLANES + tile, CONV_W), F32)],
        compiler_params=_row_params(),
        name="inproj_main",
    )(x1, g, w_in, conv_w, init)


def _inproj_extra_call(x1, g, w_in, conv_w, p1, p2, m1, m2):
    rows = x1.shape[0]
    sd = jax.ShapeDtypeStruct
    full = lambda cols: pl.BlockSpec((rows, cols), lambda i: (0, 0))
    return pl.pallas_call(
        _inproj_extra_kernel,
        grid=(1,),
        in_specs=[full(D_MODEL), _const_spec((1, D_MODEL)), _const_spec((D_MODEL, IN_COLS)),
                  _const_spec((3, CONV_W))] + [full(CONV_W)] * 4,
        out_specs=[full(ATTN_W)] * 3 + [full(CONV_W)] * 2,
        out_shape=[sd((rows, ATTN_W), F32)] * 3 + [sd((rows, CONV_W), BF16), sd((rows, CONV_W), F32)],
        scratch_shapes=[pltpu.VMEM((SUBLANES + rows, CONV_W), F32)],
        compiler_params=_row_params(),
        name="inproj_extra",
    )(x1, g, w_in, conv_w, p1, p2, m1, m2)


def _stack_maps(q):
    lane = lax.broadcasted_iota(jnp.int32, q.shape, 1)
    zero = jnp.zeros_like(q)
    return jnp.concatenate([jnp.where(lane < HEAD_DIM, q, zero), jnp.where(lane >= HEAD_DIM, q, zero)], axis=0)


def _prompt_attn_kernel(q_ref, k_ref, v_ref, km_ref, vm_ref, lq1_ref, lk1_ref, lq2_ref, lk2_ref, g_ref,
                        o_ref, m_sc, acc_sc):
    tq = q_ref.shape[0]
    i = pl.program_id(1)
    qs = [_stack_maps(q_ref[:, h * V_DIM:(h + 1) * V_DIM]) for h in range(HEADS)]
    all_rows = [(slice(0, 2 * tq), None)]
    half_rows = [tuple(slice(m * tq + e * KV_TILE, m * tq + (e + 1) * KV_TILE) for m in range(2)) for e in range(2)]

    def update(h, rows, s, v_aug, first):
        m_cur = jnp.max(s, axis=1, keepdims=True)
        if first:
            m_new = jnp.broadcast_to(m_cur, (s.shape[0], LANES))
            p = jnp.exp2(s - m_cur)
            acc_sc[h, rows, :] = _dot(p.astype(BF16), v_aug)
        else:
            m_prev = m_sc[h, rows, :]
            m_new = jnp.maximum(m_prev, m_cur)
            alpha = jnp.exp2(m_prev - m_new)
            reps = s.shape[1] // LANES
            p = jnp.exp2(s - jnp.concatenate([m_new] * reps, axis=1))
            acc_sc[h, rows, :] = (jnp.concatenate([alpha, alpha], axis=1) * acc_sc[h, rows, :]
                                  + _dot(p.astype(BF16), v_aug))
        m_sc[h, rows, :] = m_new

    def block(k_blk, v_blk, row_groups, first):
        for h in range(HEADS):
            k = k_blk(h)
            v_aug = jnp.concatenate([v_blk(h), jnp.ones(k.shape, BF16)], axis=1)
            for rows, mask in row_groups:
                s = _dot_nt(qs[h][rows], k)
                if mask is not None:
                    s = jnp.where(mask, s, NEG_INF)
                update(h, rows, s, v_aug, first)

    def kv_block(j):
        start = pl.multiple_of(j * KV_TILE, KV_TILE)
        return (lambda h: k_ref[pl.ds(start, KV_TILE), h * V_DIM:(h + 1) * V_DIM],
                lambda h: v_ref[pl.ds(start, KV_TILE), h * V_DIM:(h + 1) * V_DIM])

    tiles_per_q = tq // KV_TILE
    n_full = i * tiles_per_q
    k_a, v_a = kv_block(n_full)
    row = lax.broadcasted_iota(jnp.int32, (KV_TILE, KV_TILE + PAGE), 0)
    col = lax.broadcasted_iota(jnp.int32, (KV_TILE, KV_TILE + PAGE), 1)
    meta_ok = (col >= KV_TILE) & (col < KV_TILE + N_META)
    first_masks = [((col <= row + e * KV_TILE) & (col < KV_TILE)) | meta_ok for e in range(2)]
    block(lambda h: jnp.concatenate([k_a(h), km_ref[:, h * V_DIM:(h + 1) * V_DIM]], axis=0),
          lambda h: jnp.concatenate([v_a(h), vm_ref[:, h * V_DIM:(h + 1) * V_DIM]], axis=0),
          [(rows, first_masks[e]) for e in range(2) for rows in half_rows[e]], True)
    causal = (lax.broadcasted_iota(jnp.int32, (KV_TILE, KV_TILE), 1)
              <= lax.broadcasted_iota(jnp.int32, (KV_TILE, KV_TILE), 0))
    block(*kv_block(n_full + 1), [(rows, causal) for rows in half_rows[1]], False)

    rem = n_full % KV_UNROLL
    width = tiles_per_q
    while width < KV_UNROLL:
        def leftover(width=width):
            first_blk = (n_full // KV_UNROLL) * KV_UNROLL + (rem & (width - 1))
            for d in range(width):
                block(*kv_block(first_blk + d), all_rows, False)
        pl.when(rem & width != 0)(leftover)
        width *= 2

    def full_group(jj, carry):
        for d in range(KV_UNROLL):
            block(*kv_block(KV_UNROLL * jj + d), all_rows, False)
        return carry

    lax.fori_loop(0, n_full // KV_UNROLL, full_group, 0)

    lam = _lambda(lq1_ref, lk1_ref, lq2_ref, lk2_ref)
    for h in range(HEADS):
        acc = acc_sc[h]
        o = acc[:, :V_DIM] / acc[:, V_DIM:]
        o_ref[:, h * V_DIM:(h + 1) * V_DIM] = _head_out(o[:tq], o[tq:], lam, g_ref[...]).astype(o_ref.dtype)


def _prompt_attn_call(q, k, v, km, vm, lq1, lk1, lq2, lk2, subln_g):
    b, s, _ = q.shape
    small = lambda shape: pl.BlockSpec(shape, lambda b_, i: (0,) * len(shape))
    resident = pl.BlockSpec((None, s, ATTN_W), lambda b_, i: (b_, 0, 0))
    return pl.pallas_call(
        _prompt_attn_kernel,
        grid=(b, s // Q_TILE),
        in_specs=[pl.BlockSpec((None, Q_TILE, ATTN_W), lambda b_, i: (b_, i, 0)), resident, resident,
                  small((PAGE, ATTN_W)), small((PAGE, ATTN_W)),
                  small((1, HEAD_DIM)), small((1, HEAD_DIM)), small((1, HEAD_DIM)), small((1, HEAD_DIM)),
                  small((1, V_DIM))],
        out_specs=pl.BlockSpec((None, Q_TILE, ATTN_W), lambda b_, i: (b_, i, 0)),
        out_shape=jax.ShapeDtypeStruct((b, s, ATTN_W), BF16),
        scratch_shapes=[pltpu.VMEM((HEADS, 2 * Q_TILE, LANES), F32),
                        pltpu.VMEM((HEADS, 2 * Q_TILE, 2 * V_DIM), F32)],
        compiler_params=pltpu.CompilerParams(dimension_semantics=("arbitrary",) * 2,
                                             vmem_limit_bytes=VMEM_LIMIT),
        name="prompt_attn",
    )(q, k, v, km, vm, lq1, lk1, lq2, lk2, subln_g)


def _meta_attn_kernel(q_ref, k_ref, v_ref, lq1_ref, lk1_ref, lq2_ref, lk2_ref, g_ref, o_ref):
    t = q_ref.shape[0]
    qs = _stack_maps(q_ref[...].astype(BF16))
    pad = jnp.zeros((PAGE - t, V_DIM), F32)
    k = jnp.concatenate([k_ref[...], pad], axis=0).astype(BF16)
    v = jnp.concatenate([v_ref[...], pad], axis=0).astype(BF16)
    s = _dot_nt(qs, k)
    row = lax.broadcasted_iota(jnp.int32, s.shape, 0)
    col = lax.broadcasted_iota(jnp.int32, s.shape, 1)
    qrow = jnp.where(row >= t, row - t, row)
    s = jnp.where(col <= qrow, s, NEG_INF)
    p = jnp.exp(s - jnp.max(s, axis=-1, keepdims=True))
    o = _dot(p.astype(BF16), v) / jnp.sum(p, axis=-1, keepdims=True)
    lam = _lambda(lq1_ref, lk1_ref, lq2_ref, lk2_ref)
    o_ref[...] = _head_out(o[:t], o[t:], lam, g_ref[...])


def _meta_attn_call(q, k, v, lq1, lk1, lq2, lk2, subln_g):
    small = lambda shape: pl.BlockSpec(shape, lambda h: (0,) * len(shape))
    head = pl.BlockSpec((N_META, V_DIM), lambda h: (0, h))
    return pl.pallas_call(
        _meta_attn_kernel,
        grid=(HEADS,),
        in_specs=[head, head, head, small((1, HEAD_DIM)), small((1, HEAD_DIM)), small((1, HEAD_DIM)),
                  small((1, HEAD_DIM)), small((1, V_DIM))],
        out_specs=head,
        out_shape=jax.ShapeDtypeStruct((N_META, ATTN_W), F32),
        compiler_params=pltpu.CompilerParams(dimension_semantics=("arbitrary",)),
        name="meta_attn",
    )(q, k, v, lq1, lk1, lq2, lk2, subln_g)


def kernel(x_prompt, x_sample, cache_k, cache_v, state_conv, page_table, meta_tokens, ffn1_pre_g, ffn1_w_gate, ffn1_w_up, ffn1_w_down, ffn1_post_g, mix_pre_g, w_in, lambda_q1, lambda_k1, lambda_q2, lambda_k2, subln_g, conv_w, w_out, mix_post_g, ffn2_pre_g, ffn2_w_gate, ffn2_w_up, ffn2_w_down, ffn2_post_g):
    batch, seq, _ = x_prompt.shape
    n_seq, t_new, _ = x_sample.shape
    n_pool = cache_k.shape[1]
    l = 0

    bf = lambda w: w[l].astype(BF16)
    f1 = (ffn1_pre_g, bf(ffn1_w_gate), bf(ffn1_w_up), bf(ffn1_w_down), ffn1_post_g)
    f2 = (ffn2_pre_g, bf(ffn2_w_gate), bf(ffn2_w_up), bf(ffn2_w_down), ffn2_post_g)
    w_in_b, w_out_b = bf(w_in), bf(w_out)
    lam_args = (lambda_q1, lambda_k1, lambda_q2, lambda_k2, subln_g)
    ckt = jnp.transpose(cache_k[l], (0, 2, 3, 4, 1)).reshape(n_pool, ATTN_W, PAGE)
    cv = cache_v[l].reshape(n_pool, PAGE * HEADS, V_DIM)

    n_s = n_seq * t_new
    n_x = N_META + n_s
    xe = jnp.concatenate([meta_tokens, x_sample.reshape(n_s, D_MODEL)], axis=0)
    st = state_conv[l]
    zpad = lambda a, n: jnp.concatenate([a, jnp.zeros((n_seq, n, CONV_W), F32)], axis=1).reshape(n_s, CONV_W)
    zmeta = jnp.zeros((N_META, CONV_W), F32)
    p1 = jnp.concatenate([zmeta, zpad(st[:, 1:2], t_new - 1)], axis=0)
    p2 = jnp.concatenate([zmeta, zpad(st, t_new - 2)], axis=0)
    r = jnp.arange(n_x)[:, None]
    in_sample = r >= N_META
    pos = jnp.where(in_sample, (r - N_META) % t_new, r)
    m1 = jnp.broadcast_to((pos < 1).astype(F32), (n_x, CONV_W))
    m2 = jnp.broadcast_to((pos < 2).astype(F32), (n_x, CONV_W))

    xe1 = _ffn_stage_call(xe, f1, n_x, name="ffn1_extra")
    qe, ke, ve, ce, ue = _inproj_extra_call(xe1, mix_pre_g, w_in_b, conv_w[l], p1, p2, m1, m2)
    ae_meta = _meta_attn_call(qe[:N_META], ke[:N_META], ve[:N_META], *lam_args)

    rows = batch * seq
    half = n_seq // 2
    host = lambda seq0: (page_table, seq0, half, qe, ke, ve, N_META, t_new, ckt, cv, lam_args)
    xp = x_prompt.reshape(rows, D_MODEL)
    xp1, ae_s0 = _ffn_stage_call(xp, f1, HOST_ROW_TILE, host=host(0), name="ffn1")
    init = jnp.concatenate([jnp.zeros((SUBLANES - 2, CONV_W), F32), ue[N_META - 2:N_META]], axis=0)
    qp, kpt, vp4, kpb, vpb, cp, tail = _inproj_main_call(xp1, mix_pre_g, w_in_b, conv_w[l], init,
                                                        tile=ROW_TILE, rows_per_seq=seq)
    meta_pad = lambda a: jnp.concatenate([a[:N_META], jnp.zeros((PAGE - N_META, ATTN_W), F32)], axis=0).astype(BF16)
    shp = (batch, seq, ATTN_W)
    ap = _prompt_attn_call(qp.reshape(shp), kpb.reshape(shp), vpb.reshape(shp), meta_pad(ke), meta_pad(ve),
                           *lam_args)
    yp, ae_s1 = _ffn_stage_call(xp1, f2, HOST_ROW_TILE, mix=(ap.reshape(rows, ATTN_W), cp, w_out_b, mix_post_g),
                                host=host(half), name="mixout_ffn2")
    ae = jnp.concatenate([ae_meta, ae_s0, ae_s1], axis=0)
    ye = _ffn_stage_call(xe1, f2, n_x, mix=(ae, ce, w_out_b, mix_post_g), name="mixout_ffn2_extra")

    y_prompt = yp.reshape(batch, seq, D_MODEL)
    y_sample = ye[N_META:].reshape(n_seq, t_new, D_MODEL)
    kt_meta = jnp.broadcast_to(ke[:N_META].T[None], (batch, ATTN_W, N_META))
    kt_all = jnp.concatenate([kt_meta, kpt], axis=2).reshape(batch, HEADS, 2, HEAD_DIM, N_META + seq)
    k_prompt_new = jnp.transpose(kt_all, (0, 4, 1, 2, 3))[None]
    v_meta = jnp.broadcast_to(ve[:N_META].reshape(1, N_META, HEADS, V_DIM), (batch, N_META, HEADS, V_DIM))
    v_prompt_new = jnp.concatenate([v_meta, vp4.reshape(batch, seq, HEADS, V_DIM)], axis=1)[None]
    conv_prompt_new = tail[None]
    k_sample_new = ke[N_META:].reshape(1, n_seq, t_new, HEADS, 2, HEAD_DIM)
    v_sample_new = ve[N_META:].reshape(1, n_seq, t_new, HEADS, V_DIM)
    conv_sample_new = ue[N_META:].reshape(n_seq, t_new, CONV_W)[None, :, t_new - 2:, :]
    return (y_prompt, y_sample, k_prompt_new, v_prompt_new, conv_prompt_new,
            k_sample_new, v_sample_new, conv_sample_new)
```
